```python
import jax, jax.numpy as jnp
from jax import lax
import numpy as np

D_MODEL = 4096
BATCH = 2
SEQ = 8192
DEPTH = 1
DEC_BATCH = 8
DEC_SEQ = 2048
PAST_LEN = 128

N_HEADS = 16
N_KV_HEADS = 4
HEAD_DIM = 128
ATTN_WIDTH = N_HEADS * HEAD_DIM
KV_WIDTH = N_KV_HEADS * HEAD_DIM
Q_BLOCK = 128
ROPE_THETA = 10000.0
GRID_W = 64
LRU_WIDTH = 2048
LRU_BLOCKS = 16
LRU_BLOCK_W = LRU_WIDTH // LRU_BLOCKS
CONV_W = 4
CONV_LEFT = 2
LRU_C = 8.0
LRU_A_MIN = 0.9
LRU_A_MAX = 0.999
PEER_HEADS = 8
PEER_N_KEYS = 128
PEER_N_EXPERTS = PEER_N_KEYS * PEER_N_KEYS
PEER_HALF = 128
PEER_QUERY_DIM = 2 * PEER_HALF
PEER_TOPK = 16
PEER_TOKEN_CHUNK = 128
N_MOD = 6
EPS = 1e-6
IN_COLS = ATTN_WIDTH + 2 * KV_WIDTH + 2 * LRU_WIDTH + 2 * D_MODEL

kernel_name = 'hybrid_rglru_gqa_peer_encoder'


def rmsnorm(x, g):
    xf = x.astype(jnp.float32)
    y = xf * lax.rsqrt(jnp.mean(xf * xf, axis=-1, keepdims=True) + EPS)
    return (y * g.astype(jnp.float32)).astype(x.dtype)


def modulate(h, shift, scale):
    return h * (1 + scale[:, None, :]) + shift[:, None, :]


def axial_rope_tables(seq_len, dtype):
    rows_count = seq_len // GRID_W
    rows = jnp.repeat(jnp.arange(rows_count), GRID_W)
    cols = jnp.tile(jnp.arange(GRID_W), rows_count)
    axis_dim = HEAD_DIM // 2
    inv_freq = ROPE_THETA ** (-jnp.arange(0, axis_dim, 2, dtype=jnp.float32) / axis_dim)
    ang_r = rows.astype(jnp.float32)[:, None] * inv_freq
    ang_c = cols.astype(jnp.float32)[:, None] * inv_freq
    f = lambda a: a[None, :, None, :].astype(dtype)
    return f(jnp.cos(ang_r)), f(jnp.sin(ang_r)), f(jnp.cos(ang_c)), f(jnp.sin(ang_c))


def rope_rotate(p, cos, sin):
    p1, p2 = jnp.split(p, 2, axis=-1)
    return jnp.concatenate([p1 * cos - p2 * sin, p2 * cos + p1 * sin], axis=-1)


def apply_axial_rope(x, tables):
    cos_r, sin_r, cos_c, sin_c = tables
    xr, xc = jnp.split(x, 2, axis=-1)
    return jnp.concatenate([rope_rotate(xr, cos_r, sin_r), rope_rotate(xc, cos_c, sin_c)], axis=-1)


def gqa_attention(q, k, v):
    B, S = q.shape[0], q.shape[1]
    G = N_HEADS // N_KV_HEADS
    nb = S // Q_BLOCK
    qb = q.reshape(B, nb, Q_BLOCK, N_KV_HEADS, G, HEAD_DIM).transpose(1, 0, 2, 3, 4, 5)
    scale = HEAD_DIM ** -0.5

    def one_block(qblk):
        s = jnp.einsum('bqkgd,bskd->bkgqs', qblk, k, preferred_element_type=jnp.float32) * scale
        p = jax.nn.softmax(s, axis=-1)
        return jnp.einsum('bkgqs,bskd->bqkgd', p.astype(v.dtype), v)

    o = lax.map(one_block, qb)
    return o.transpose(1, 0, 2, 3, 4, 5).reshape(B, S, ATTN_WIDTH)


def centred_depthwise_conv(x, w, b):
    S = x.shape[1]
    xp = jnp.pad(x, ((0, 0), (CONV_LEFT, CONV_W - 1 - CONV_LEFT), (0, 0)))
    out = b + xp[:, 0:S] * w[0]
    for j in range(1, CONV_W):
        out = out + xp[:, j:j + S] * w[j]
    return out


def block_diag(xb, w, b):
    B, S = xb.shape[0], xb.shape[1]
    return jnp.einsum('bsnc,ncd->bsnd', xb, w.astype(jnp.float32)).reshape(B, S, LRU_WIDTH) + b.astype(jnp.float32)


def rg_lru(x, lam, w_a, b_a, w_i, b_i, reverse):
    B, S, W = x.shape
    xf = x.astype(jnp.float32)
    xb = xf.reshape(B, S, LRU_BLOCKS, LRU_BLOCK_W)
    r = jax.nn.sigmoid(block_diag(xb, w_a, b_a))
    i = jax.nn.sigmoid(block_diag(xb, w_i, b_i))
    log_a = -LRU_C * r * jax.nn.softplus(-lam.astype(jnp.float32))
    a = jnp.exp(log_a)
    mult = jnp.sqrt(jnp.maximum(-jnp.expm1(2.0 * log_a), 0.0))
    u = mult * (i * xf)

    def combine(left, right):
        a1, b1 = left
        a2, b2 = right
        return a1 * a2, a2 * b1 + b2

    _, h = lax.associative_scan(combine, (a, u), axis=1, reverse=reverse)
    return h


def peer_ffn(h, w_q, sub_keys, expert_u, expert_v):
    B, S, D = h.shape
    T = B * S
    hf = h.reshape(T, D)
    q = (hf @ w_q).reshape(T, PEER_HEADS, 2, PEER_HALF)
    scores = jnp.einsum('thpc,hpnc->thpn', q, sub_keys, preferred_element_type=jnp.float32)
    s_top, i_top = lax.top_k(scores, PEER_TOPK)
    cand = (s_top[:, :, 0, :, None] + s_top[:, :, 1, None, :]).reshape(T, PEER_HEADS, PEER_TOPK * PEER_TOPK)
    cand_idx = (i_top[:, :, 0, :, None] * PEER_N_KEYS + i_top[:, :, 1, None, :]).reshape(T, PEER_HEADS, PEER_TOPK * PEER_TOPK)
    best_s, best_pos = lax.top_k(cand, PEER_TOPK)
    expert_idx = jnp.take_along_axis(cand_idx, best_pos, axis=-1)
    gates = jax.nn.softmax(best_s, axis=-1)
    HK = PEER_HEADS * PEER_TOPK
    n_chunks = T // PEER_TOKEN_CHUNK

    def chunk(args):
        xc, idx, g = args
        u = jnp.take(expert_u, idx, axis=0)
        act = jax.nn.gelu(jnp.einsum('cd,ced->ce', xc, u, preferred_element_type=jnp.float32), approximate=False)
        coef = (g * act).astype(expert_v.dtype)
        v = jnp.take(expert_v, idx, axis=0)
        return jnp.einsum('ce,ced->cd', coef, v)

    out = lax.map(chunk, (hf.reshape(n_chunks, PEER_TOKEN_CHUNK, D),
                          expert_idx.reshape(n_chunks, PEER_TOKEN_CHUNK, HK),
                          gates.reshape(n_chunks, PEER_TOKEN_CHUNK, HK)))
    return out.reshape(B, S, D).astype(h.dtype)


def encoder_layer(x, c, w_ada, b_ada, g_norm1, g_norm2, w_in, q_gain, k_gain, conv_w, conv_b,
                  lru_lam, lru_wa, lru_ba, lru_wi, lru_bi, w_attn_o, w_lru_o, w_out,
                  peer_wq, peer_keys, peer_u, peer_v):
    B, S, _ = x.shape
    mod = jax.nn.silu(c) @ w_ada + b_ada
    shift1, scale1, gate1, shift2, scale2, gate2 = jnp.split(mod, N_MOD, axis=-1)

    h = modulate(rmsnorm(x, g_norm1), shift1, scale1)
    proj = h @ w_in
    c0 = ATTN_WIDTH
    c1 = c0 + KV_WIDTH
    c2 = c1 + KV_WIDTH
    c3 = c2 + LRU_WIDTH
    c4 = c3 + LRU_WIDTH
    c5 = c4 + D_MODEL
    q, k, v, xl, yl, g_attn, g_lru = jnp.split(proj, [c0, c1, c2, c3, c4, c5], axis=-1)

    tables = axial_rope_tables(S, x.dtype)
    q = apply_axial_rope(rmsnorm(q.reshape(B, S, N_HEADS, HEAD_DIM), q_gain), tables)
    k = apply_axial_rope(rmsnorm(k.reshape(B, S, N_KV_HEADS, HEAD_DIM), k_gain), tables)
    v = v.reshape(B, S, N_KV_HEADS, HEAD_DIM)
    attn = gqa_attention(q, k, v)

    xc = centred_depthwise_conv(xl, conv_w, conv_b)
    hl = (rg_lru(xc, lru_lam[0], lru_wa[0], lru_ba[0], lru_wi[0], lru_bi[0], False)
          + rg_lru(xc, lru_lam[1], lru_wa[1], lru_ba[1], lru_wi[1], lru_bi[1], True))
    lru = (hl * jax.nn.gelu(yl.astype(jnp.float32), approximate=False)).astype(x.dtype)

    merged = jax.nn.sigmoid(g_attn) * (attn @ w_attn_o) + jax.nn.sigmoid(g_lru) * (lru @ w_lru_o)
    x = x + gate1[:, None, :] * (merged @ w_out)

    h2 = modulate(rmsnorm(x, g_norm2), shift2, scale2)
    x = x + gate2[:, None, :] * peer_ffn(h2, peer_wq, peer_keys, peer_u, peer_v)
    return x


def setup_inputs(seed: int = 0) -> dict:
    key = jax.random.key(seed)
    ks = jax.random.split(key, 32)
    nrm = lambda k, shape, s: jax.random.normal(k, shape, jnp.float32) * s
    a0 = jax.random.uniform(ks[12], (DEPTH, 2, LRU_WIDTH), jnp.float32, LRU_A_MIN, LRU_A_MAX)
    return {
        'x_prompt': nrm(ks[0], (BATCH, SEQ, D_MODEL), 1.0),
        'x_sample': nrm(ks[1], (DEC_BATCH, DEC_SEQ, D_MODEL), 1.0),
        'c_prompt': nrm(ks[2], (BATCH, D_MODEL), 1.0),
        'c_sample': nrm(ks[3], (DEC_BATCH, D_MODEL), 1.0),
        'w_ada': nrm(ks[4], (DEPTH, D_MODEL, N_MOD * D_MODEL), 0.5 * D_MODEL ** -0.5),
        'b_ada': nrm(ks[5], (DEPTH, N_MOD * D_MODEL), 0.01),
        'g_norm1': 1.0 + nrm(ks[6], (DEPTH, D_MODEL), 0.02),
        'g_norm2': 1.0 + nrm(ks[7], (DEPTH, D_MODEL), 0.02),
        'w_in': nrm(ks[8], (DEPTH, D_MODEL, IN_COLS), D_MODEL ** -0.5),
        'q_gain': 1.0 + nrm(ks[9], (DEPTH, HEAD_DIM), 0.02),
        'k_gain': 1.0 + nrm(ks[10], (DEPTH, HEAD_DIM), 0.02),
        'conv_w': nrm(ks[11], (DEPTH, CONV_W, LRU_WIDTH), CONV_W ** -0.5),
        'conv_b': nrm(ks[13], (DEPTH, LRU_WIDTH), 0.01),
        'lru_lam': jnp.log(a0) - jnp.log1p(-a0),
        'lru_wa': nrm(ks[14], (DEPTH, 2, LRU_BLOCKS, LRU_BLOCK_W, LRU_BLOCK_W), LRU_BLOCK_W ** -0.5),
        'lru_ba': nrm(ks[15], (DEPTH, 2, LRU_WIDTH), 0.01),
        'lru_wi': nrm(ks[16], (DEPTH, 2, LRU_BLOCKS, LRU_BLOCK_W, LRU_BLOCK_W), LRU_BLOCK_W ** -0.5),
        'lru_bi': nrm(ks[17], (DEPTH, 2, LRU_WIDTH), 0.01),
        'w_attn_o': nrm(ks[18], (DEPTH, ATTN_WIDTH, D_MODEL), ATTN_WIDTH ** -0.5),
        'w_lru_o': nrm(ks[19], (DEPTH, LRU_WIDTH, D_MODEL), LRU_WIDTH ** -0.5),
        'w_out': nrm(ks[20], (DEPTH, D_MODEL, D_MODEL), D_MODEL ** -0.5),
        'peer_wq': nrm(ks[21], (DEPTH, D_MODEL, PEER_HEADS * PEER_QUERY_DIM), D_MODEL ** -0.5),
        'peer_keys': nrm(ks[22], (DEPTH, PEER_HEADS, 2, PEER_N_KEYS, PEER_HALF), PEER_HALF ** -0.5),
        'peer_u': nrm(ks[23], (DEPTH, PEER_N_EXPERTS, D_MODEL), D_MODEL ** -0.5),
        'peer_v': nrm(ks[24], (DEPTH, PEER_N_EXPERTS, D_MODEL), 0.5),
    }


def reference(x_prompt, x_sample, c_prompt, c_sample, w_ada, b_ada, g_norm1, g_norm2, w_in,
              q_gain, k_gain, conv_w, conv_b, lru_lam, lru_wa, lru_ba, lru_wi, lru_bi,
              w_attn_o, w_lru_o, w_out, peer_wq, peer_keys, peer_u, peer_v):
    y_prompt = x_prompt
    y_sample = x_sample
    for l in range(DEPTH):
        layer_params = (w_ada[l], b_ada[l], g_norm1[l], g_norm2[l], w_in[l], q_gain[l], k_gain[l],
                        conv_w[l], conv_b[l], lru_lam[l], lru_wa[l], lru_ba[l], lru_wi[l], lru_bi[l],
                        w_attn_o[l], w_lru_o[l], w_out[l], peer_wq[l], peer_keys[l], peer_u[l], peer_v[l])
        y_prompt = encoder_layer(y_prompt, c_prompt, *layer_params)
        y_sample = encoder_layer(y_sample, c_sample, *layer_params)
    return (y_prompt, y_sample)
```

```python
import functools
import math

import jax
import jax.numpy as jnp
from jax import lax
from jax.experimental import pallas as pl
from jax.experimental.pallas import tpu as pltpu

F32 = jnp.float32
BF16 = jnp.bfloat16

EPS = 1e-6
ROPE_THETA = 10000.0
GRID_W = 64
LRU_C = 8.0
CONV_LEFT = 2
PEER_TOPK = 16
N_MOD = 6

V7X_VMEM_BYTES = 64 * 1024 * 1024
V7X_VMEM_RESERVE = 8 * 1024 * 1024
LANES = 128
SUBLANES = 8

LRU_TIME_CHUNK = 1024
ATTN_Q_TILE = 256
ATTN_KV_TILE = 512

_CONTRACT_LAST = (((1,), (1,)), ((), ()))
_CONTRACT_FIRST = (((0,), (0,)), ((), ()))


def _cparams(semantics, vmem_bytes):
    limit = min(int(vmem_bytes), V7X_VMEM_BYTES - V7X_VMEM_RESERVE)
    return pltpu.CompilerParams(dimension_semantics=semantics, vmem_limit_bytes=limit)


def _tile(n, pref, *also):
    g = math.gcd(n, *also)
    for t in range(min(pref, g) // LANES * LANES, 0, -LANES):
        if g % t == 0:
            return t
    return g


def _nbytes(shape, dtype):
    return math.prod(shape) * jnp.dtype(dtype).itemsize


def _ada_kernel(c_ref, w_ref, b_ref, o_ref):
    c = c_ref[...]
    s = c * jax.nn.sigmoid(c)
    o_ref[...] = jnp.dot(s, w_ref[...], preferred_element_type=F32,
                         precision=lax.Precision.HIGHEST) + b_ref[...]


def _adaln(c_rows, w_ada, b_ada):
    R, D = c_rows.shape
    N = w_ada.shape[1]
    tn = _tile(N, 512)
    vmem = 2 * (_nbytes((R, D), F32) + _nbytes((D, tn), F32) + 2 * _nbytes((R, tn), F32)) \
        + 3 * _nbytes((D, tn), F32)
    return pl.pallas_call(
        _ada_kernel,
        grid=(N // tn,),
        in_specs=[pl.BlockSpec((R, D), lambda j: (0, 0)),
                  pl.BlockSpec((D, tn), lambda j: (0, j)),
                  pl.BlockSpec((1, tn), lambda j: (0, j))],
        out_specs=pl.BlockSpec((R, tn), lambda j: (0, j)),
        out_shape=jax.ShapeDtypeStruct((R, N), F32),
        compiler_params=_cparams(("parallel",), vmem),
        name="adaln",
    )(c_rows, w_ada, b_ada.reshape(1, N))


def _modnorm_kernel(x_ref, g_ref, sh_ref, sc_ref, o_ref):
    x = x_ref[0]
    y = x * lax.rsqrt(jnp.mean(x * x, axis=-1, keepdims=True) + EPS) * g_ref[...]
    o_ref[0] = (y * (1.0 + sc_ref[0]) + sh_ref[0]).astype(o_ref.dtype)


def _modnorm(x, mod3, r0, g, k_shift, k_scale):
    B, S, D = x.shape
    ts = _tile(S, 512)
    vmem = 2 * (_nbytes((ts, D), F32) + _nbytes((ts, D), BF16)) + 3 * _nbytes((ts, D), F32)
    return pl.pallas_call(
        _modnorm_kernel,
        grid=(B, S // ts),
        in_specs=[pl.BlockSpec((1, ts, D), lambda b, s: (b, s, 0)),
                  pl.BlockSpec((1, D), lambda b, s: (0, 0)),
                  pl.BlockSpec((1, 1, D), lambda b, s: (r0 + b, 0, k_shift)),
                  pl.BlockSpec((1, 1, D), lambda b, s: (r0 + b, 0, k_scale))],
        out_specs=pl.BlockSpec((1, ts, D), lambda b, s: (b, s, 0)),
        out_shape=jax.ShapeDtypeStruct((B, S, D), BF16),
        compiler_params=_cparams(("parallel", "parallel"), vmem),
        name="modnorm",
    )(x, g.reshape(1, D), mod3, mod3)


def _qkv_kernel(h_ref, w_ref, gq_ref, gk_ref, cos_ref, sin_ref, o_ref, *, n_q, n_qk, hd):
    j = pl.program_id(2)
    acc = jnp.dot(h_ref[0], w_ref[...], preferred_element_type=F32)
    tn = acc.shape[1]

    @pl.when(j < n_qk)
    def _():
        gain = jnp.where(j < n_q, gq_ref[...], gk_ref[...])
        cos = cos_ref[...]
        sin = sin_ref[...]
        lane = lax.broadcasted_iota(jnp.int32, (1, hd), 1)
        first = (lane % (hd // 2)) < (hd // 4)
        for hh in range(tn // hd):
            y = acc[:, hh * hd:(hh + 1) * hd]
            y = y * lax.rsqrt(jnp.mean(y * y, axis=-1, keepdims=True) + EPS) * gain
            sw = jnp.where(first, pltpu.roll(y, hd - hd // 4, 1), pltpu.roll(y, hd // 4, 1))
            o_ref[0, :, hh * hd:(hh + 1) * hd] = (y * cos + sw * sin).astype(o_ref.dtype)

    @pl.when(j >= n_qk)
    def _():
        o_ref[0] = acc.astype(o_ref.dtype)


def _qkv_proj(h, w_in_bf, gq, gk, cos_t, sin_t, attn_w, kv_w, hd):
    B, S, D = h.shape
    n_cols = attn_w + 2 * kv_w
    tn = _tile(attn_w, 512, kv_w)
    tm = _tile(S, 1024)
    vmem = 2 * (_nbytes((tm, D), BF16) + _nbytes((D, tn), BF16) + _nbytes((tm, tn), BF16)
                + 2 * _nbytes((tm, hd), F32)) + 4 * _nbytes((tm, tn), F32)
    kern = functools.partial(_qkv_kernel, n_q=attn_w // tn, n_qk=(attn_w + kv_w) // tn, hd=hd)
    return pl.pallas_call(
        kern,
        grid=(B, S // tm, n_cols // tn),
        in_specs=[pl.BlockSpec((1, tm, D), lambda b, i, j: (b, i, 0)),
                  pl.BlockSpec((D, tn), lambda b, i, j: (0, j)),
                  pl.BlockSpec((1, hd), lambda b, i, j: (0, 0)),
                  pl.BlockSpec((1, hd), lambda b, i, j: (0, 0)),
                  pl.BlockSpec((tm, hd), lambda b, i, j: (i, 0)),
                  pl.BlockSpec((tm, hd), lambda b, i, j: (i, 0))],
        out_specs=pl.BlockSpec((1, tm, tn), lambda b, i, j: (b, i, j)),
        out_shape=jax.ShapeDtypeStruct((B, S, n_cols), BF16),
        compiler_params=_cparams(("parallel", "parallel", "arbitrary"), vmem),
        name="qkv_proj",
    )(h, w_in_bf, gq, gk, cos_t, sin_t)


def _proj2_kernel(h_ref, w_ref, o_ref, *, n_plain):
    j = pl.program_id(2)
    acc = jnp.dot(h_ref[0], w_ref[...], preferred_element_type=F32)

    @pl.when(j < n_plain)
    def _():
        o_ref[0] = acc

    @pl.when(j >= n_plain)
    def _():
        o_ref[0] = jax.nn.sigmoid(acc)


def _proj2(h, w_in_bf, col0, lru_w):
    B, S, D = h.shape
    n_cols = 2 * lru_w + 2 * D
    tn = _tile(2 * lru_w, 1024, 2 * D, col0)
    tm = _tile(S, 1024)
    j0 = col0 // tn
    vmem = 2 * (_nbytes((tm, D), BF16) + _nbytes((D, tn), BF16) + _nbytes((tm, tn), F32)) \
        + 4 * _nbytes((tm, tn), F32)
    return pl.pallas_call(
        functools.partial(_proj2_kernel, n_plain=2 * lru_w // tn),
        grid=(B, S // tm, n_cols // tn),
        in_specs=[pl.BlockSpec((1, tm, D), lambda b, i, j: (b, i, 0)),
                  pl.BlockSpec((D, tn), lambda b, i, j: (0, j0 + j))],
        out_specs=pl.BlockSpec((1, tm, tn), lambda b, i, j: (b, i, j)),
        out_shape=jax.ShapeDtypeStruct((B, S, n_cols), F32),
        compiler_params=_cparams(("parallel", "parallel", "arbitrary"), vmem),
        name="proj2",
    )(h, w_in_bf)


def _attn_kernel(q_ref, k_ref, v_ref, o_ref, *, G, hd, tk):
    tq = q_ref.shape[1]
    nk = k_ref.shape[1] // tk
    for g in range(G):
        q = q_ref[0, :, g * hd:(g + 1) * hd]

        def body(c, carry, q=q):
            m, l, acc = carry
            off = pl.multiple_of(c * tk, tk)
            k = k_ref[0, pl.ds(off, tk), :]
            v = v_ref[0, pl.ds(off, tk), :]
            s = lax.dot_general(q, k, _CONTRACT_LAST, preferred_element_type=F32)
            m_new = jnp.maximum(m, jnp.max(s, axis=-1, keepdims=True))
            alpha = jnp.exp(m - m_new)
            p = jnp.exp(s - m_new)
            l = alpha * l + jnp.sum(p, axis=-1, keepdims=True)
            acc = alpha * acc + jnp.dot(p.astype(v.dtype), v, preferred_element_type=F32)
            return m_new, l, acc

        init = (jnp.full((tq, 1), -jnp.inf, F32), jnp.zeros((tq, 1), F32), jnp.zeros((tq, hd), F32))
        _, l, acc = lax.fori_loop(0, nk, body, init)
        o_ref[0, :, g * hd:(g + 1) * hd] = (acc / l).astype(o_ref.dtype)


def _attention(qkv, attn_w, kv_w, hd):
    B, S, _ = qkv.shape
    kvh = kv_w // hd
    G = attn_w // kv_w
    tq = _tile(S, ATTN_Q_TILE)
    tk = _tile(S, ATTN_KV_TILE)
    k0 = attn_w // hd
    v0 = (attn_w + kv_w) // hd
    vmem = 2 * (2 * _nbytes((tq, G * hd), BF16) + 2 * _nbytes((S, hd), BF16)) + 6 * _nbytes((tq, tk), F32)
    return pl.pallas_call(
        functools.partial(_attn_kernel, G=G, hd=hd, tk=tk),
        grid=(B, kvh, S // tq),
        in_specs=[pl.BlockSpec((1, tq, G * hd), lambda b, n, i: (b, i, n)),
                  pl.BlockSpec((1, S, hd), lambda b, n, i: (b, 0, k0 + n)),
                  pl.BlockSpec((1, S, hd), lambda b, n, i: (b, 0, v0 + n))],
        out_specs=pl.BlockSpec((1, tq, G * hd), lambda b, n, i: (b, i, n)),
        out_shape=jax.ShapeDtypeStruct((B, S, attn_w), BF16),
        compiler_params=_cparams(("parallel", "parallel", "arbitrary"), vmem),
        name="attention",
    )(qkv, qkv, qkv)


def _lru_kernel(*refs, reverse, n_chunks, conv_taps):
    if reverse:
        (cur_ref, prev_ref, next_ref, cw_ref, cb_ref, w_ref, b_ref, lam_ref, hf_ref, yl_ref,
         o_ref, ext_scr, a_scr, u_scr, carry_scr) = refs
    else:
        (cur_ref, prev_ref, next_ref, cw_ref, cb_ref, w_ref, b_ref, lam_ref,
         o_ref, ext_scr, a_scr, u_scr, carry_scr) = refs
    tc, wc = a_scr.shape
    bw = w_ref.shape[1]
    s = pl.program_id(2)
    chunk = (n_chunks - 1 - s) if reverse else s

    ext_scr[0:SUBLANES, :] = jnp.where(chunk > 0, prev_ref[0], 0.0)
    ext_scr[SUBLANES:SUBLANES + tc, :] = cur_ref[0]
    ext_scr[SUBLANES + tc:, :] = jnp.where(chunk < n_chunks - 1, next_ref[0], 0.0)
    xc = cb_ref[...]
    for j in range(conv_taps):
        r0 = SUBLANES - CONV_LEFT + j
        xc = xc + ext_scr[r0:r0 + tc, :] * cw_ref[j:j + 1, :]

    lam = -lam_ref[...]
    softplus_neg_lam = jnp.maximum(lam, 0.0) + jnp.log1p(jnp.exp(-jnp.abs(lam)))
    for n in range(wc // bw):
        cols = slice(n * bw, (n + 1) * bw)
        xb = xc[:, cols]
        z = jnp.dot(xb.astype(BF16), w_ref[n], preferred_element_type=F32)
        r = jax.nn.sigmoid(z[:, :bw] + b_ref[0:1, cols])
        i = jax.nn.sigmoid(z[:, bw:] + b_ref[1:2, cols])
        log_a = -LRU_C * r * softplus_neg_lam[:, cols]
        a = jnp.exp(log_a)
        mult = jnp.sqrt(jnp.maximum(-jnp.tanh(log_a) * (a * a + 1.0), 0.0))
        a_scr[:, cols] = a
        u_scr[:, cols] = mult * (i * xb)

    @pl.when(s == 0)
    def _():
        carry_scr[...] = jnp.zeros_like(carry_scr)

    row = lax.broadcasted_iota(jnp.int32, (SUBLANES, wc), 0)
    n_tiles = tc // SUBLANES

    def tile_body(t, h_in):
        tile = (n_tiles - 1 - t) if reverse else t
        r0 = pl.multiple_of(tile * SUBLANES, SUBLANES)
        a = a_scr[pl.ds(r0, SUBLANES), :]
        u = u_scr[pl.ds(r0, SUBLANES), :]
        for d in (1, 2, 4):
            if reverse:
                valid = row < SUBLANES - d
                shift = SUBLANES - d
            else:
                valid = row >= d
                shift = d
            u = u + a * jnp.where(valid, pltpu.roll(u, shift, 0), 0.0)
            a = a * jnp.where(valid, pltpu.roll(a, shift, 0), 1.0)
        h = u + a * h_in
        u_scr[pl.ds(r0, SUBLANES), :] = h
        edge = h[0:1, :] if reverse else h[SUBLANES - 1:SUBLANES, :]
        return jnp.broadcast_to(edge, (SUBLANES, wc))

    carry_scr[...] = lax.fori_loop(0, n_tiles, tile_body, carry_scr[...], unroll=2)

    if reverse:
        y = yl_ref[0]
        gelu = 0.5 * y * (1.0 + lax.erf(y * (2.0 ** -0.5)))
        o_ref[0] = ((hf_ref[0] + u_scr[...]) * gelu).astype(o_ref.dtype)
    else:
        o_ref[0] = u_scr[...]


def _lru(proj2, conv_w, conv_b, w_gate, b_gate, lam, lru_w, *, reverse, hf=None):
    B, S, _ = proj2.shape
    nb, bw, _ = w_gate.shape
    taps = conv_w.shape[0]
    wc = _tile(lru_w, 512)
    tc = _tile(S, LRU_TIME_CHUNK)
    n_chunks = S // tc
    n_wc = lru_w // wc
    halo_blocks = S // SUBLANES
    per_chunk = tc // SUBLANES

    def t_idx(s):
        return (n_chunks - 1 - s) if reverse else s

    in_specs = [
        pl.BlockSpec((1, tc, wc), lambda b, c, s: (b, t_idx(s), c)),
        pl.BlockSpec((1, SUBLANES, wc), lambda b, c, s: (b, jnp.maximum(t_idx(s) * per_chunk - 1, 0), c)),
        pl.BlockSpec((1, SUBLANES, wc),
                     lambda b, c, s: (b, jnp.minimum((t_idx(s) + 1) * per_chunk, halo_blocks - 1), c)),
        pl.BlockSpec((taps, wc), lambda b, c, s: (0, c)),
        pl.BlockSpec((1, wc), lambda b, c, s: (0, c)),
        pl.BlockSpec((wc // bw, bw, 2 * bw), lambda b, c, s: (c, 0, 0)),
        pl.BlockSpec((2, wc), lambda b, c, s: (0, c)),
        pl.BlockSpec((1, wc), lambda b, c, s: (0, c)),
    ]
    args = [proj2, proj2, proj2, conv_w, conv_b.reshape(1, lru_w), w_gate, b_gate, lam.reshape(1, lru_w)]
    if reverse:
        in_specs += [pl.BlockSpec((1, tc, wc), lambda b, c, s: (b, t_idx(s), c)),
                     pl.BlockSpec((1, tc, wc), lambda b, c, s: (b, t_idx(s), n_wc + c))]
        args += [hf, proj2]
        out_dtype = BF16
    else:
        out_dtype = F32
    blk = _nbytes((tc, wc), F32)
    vmem = 2 * (4 * blk) + 3 * blk + 10 * blk
    return pl.pallas_call(
        functools.partial(_lru_kernel, reverse=reverse, n_chunks=n_chunks, conv_taps=taps),
        grid=(B, n_wc, n_chunks),
        in_specs=in_specs,
        out_specs=pl.BlockSpec((1, tc, wc), lambda b, c, s: (b, t_idx(s), c)),
        out_shape=jax.ShapeDtypeStruct((B, S, lru_w), out_dtype),
        scratch_shapes=[pltpu.VMEM((tc + 2 * SUBLANES, wc), F32),
                        pltpu.VMEM((tc, wc), F32),
                        pltpu.VMEM((tc, wc), F32),
                        pltpu.VMEM((SUBLANES, wc), F32)],
        compiler_params=_cparams(("parallel", "parallel", "arbitrary"), vmem),
        name="lru_bwd" if reverse else "lru_fwd",
    )(*args)


def _merge_kernel(attn_ref, lru_ref, wa_ref, wl_ref, ga_ref, gl_ref, o_ref):
    a = jnp.dot(attn_ref[0], wa_ref[...], preferred_element_type=F32)
    l = jnp.dot(lru_ref[0], wl_ref[...], preferred_element_type=F32)
    o_ref[0] = (ga_ref[0] * a + gl_ref[0] * l).astype(o_ref.dtype)


def _merge(attn, lru, w_attn_o_bf, w_lru_o_bf, proj2, lru_w):
    B, S, attn_w = attn.shape
    D = w_attn_o_bf.shape[1]
    tm = _tile(S, 1024)
    tn = _tile(D, 512, 2 * lru_w)
    ga0 = 2 * lru_w // tn
    gl0 = (2 * lru_w + D) // tn
    vmem = 2 * (_nbytes((tm, attn_w), BF16) + _nbytes((tm, lru_w), BF16) + _nbytes((attn_w, tn), BF16)
                + _nbytes((lru_w, tn), BF16) + 2 * _nbytes((tm, tn), F32) + _nbytes((tm, tn), BF16)) \
        + 3 * _nbytes((tm, tn), F32)
    return pl.pallas_call(
        _merge_kernel,
        grid=(B, S // tm, D // tn),
        in_specs=[pl.BlockSpec((1, tm, attn_w), lambda b, i, j: (b, i, 0)),
                  pl.BlockSpec((1, tm, lru_w), lambda b, i, j: (b, i, 0)),
                  pl.BlockSpec((attn_w, tn), lambda b, i, j: (0, j)),
                  pl.BlockSpec((lru_w, tn), lambda b, i, j: (0, j)),
                  pl.BlockSpec((1, tm, tn), lambda b, i, j: (b, i, ga0 + j)),
                  pl.BlockSpec((1, tm, tn), lambda b, i, j: (b, i, gl0 + j))],
        out_specs=pl.BlockSpec((1, tm, tn), lambda b, i, j: (b, i, j)),
        out_shape=jax.ShapeDtypeStruct((B, S, D), BF16),
        compiler_params=_cparams(("parallel", "parallel", "arbitrary"), vmem),
        name="merge",
    )(attn, lru, w_attn_o_bf, w_lru_o_bf, proj2, proj2)


def _outproj_kernel(m_ref, w_ref, x_ref, g_ref, o_ref):
    o_ref[0] = x_ref[0] + g_ref[0] * jnp.dot(m_ref[0], w_ref[...], preferred_element_type=F32)


def _outproj(merged, w_out_bf, x, mod3, r0, k_gate):
    B, S, D = x.shape
    tm = _tile(S, 1024)
    tn = _tile(D, 1024)
    gpb = D // tn
    vmem = 2 * (_nbytes((tm, D), BF16) + _nbytes((D, tn), BF16) + 2 * _nbytes((tm, tn), F32)) \
        + 2 * _nbytes((tm, tn), F32)
    return pl.pallas_call(
        _outproj_kernel,
        grid=(B, S // tm, D // tn),
        in_specs=[pl.BlockSpec((1, tm, D), lambda b, i, j: (b, i, 0)),
                  pl.BlockSpec((D, tn), lambda b, i, j: (0, j)),
                  pl.BlockSpec((1, tm, tn), lambda b, i, j: (b, i, j)),
                  pl.BlockSpec((1, 1, tn), lambda b, i, j: (r0 + b, 0, k_gate * gpb + j))],
        out_specs=pl.BlockSpec((1, tm, tn), lambda b, i, j: (b, i, j)),
        out_shape=jax.ShapeDtypeStruct((B, S, D), F32),
        compiler_params=_cparams(("parallel", "parallel", "arbitrary"), vmem),
        name="outproj",
    )(merged, w_out_bf, x, mod3)


def _matmul_kernel(a_ref, w_ref, o_ref):
    o_ref[...] = jnp.dot(a_ref[...], w_ref[...], preferred_element_type=F32).astype(o_ref.dtype)


def _matmul(a, w, out_dtype):
    M, K = a.shape
    N = w.shape[1]
    tm = _tile(M, 1024)
    tn = _tile(N, 1024)
    vmem = 2 * (_nbytes((tm, K), a.dtype) + _nbytes((K, tn), w.dtype) + _nbytes((tm, tn), out_dtype)) \
        + 2 * _nbytes((tm, tn), F32)
    return pl.pallas_call(
        _matmul_kernel,
        grid=(M // tm, N // tn),
        in_specs=[pl.BlockSpec((tm, K), lambda i, j: (i, 0)),
                  pl.BlockSpec((K, tn), lambda i, j: (0, j))],
        out_specs=pl.BlockSpec((tm, tn), lambda i, j: (i, j)),
        out_shape=jax.ShapeDtypeStruct((M, N), out_dtype),
        compiler_params=_cparams(("parallel", "arbitrary"), vmem),
        name="peer_query",
    )(a, w)


def _batcher_pairs(n):
    pairs = []

    def merge(lo, hi, r):
        step = r * 2
        if step < hi - lo:
            merge(lo, hi, step)
            merge(lo + r, hi, step)
            pairs.extend((i, i + r) for i in range(lo + r, hi - r, step))
        else:
            pairs.append((lo, lo + r))

    def sort(lo, hi):
        if hi - lo >= 1:
            mid = lo + (hi - lo) // 2
            sort(lo, mid)
            sort(mid + 1, hi)
            merge(lo, hi, 1)

    sort(0, n - 1)
    return pairs


_SORT_PAIRS = _batcher_pairs(PEER_TOPK)


def _sort_desc(v):
    v = list(v)
    for i, j in _SORT_PAIRS:
        v[i], v[j] = jnp.maximum(v[i], v[j]), jnp.minimum(v[i], v[j])
    return v


def _merge_top(a, b):
    k = len(a)
    c = [jnp.maximum(a[i], b[k - 1 - i]) for i in range(k)]
    d = k // 2
    while d >= 1:
        for i in range(k):
            if not i & d:
                c[i], c[i + d] = jnp.maximum(c[i], c[i + d]), jnp.minimum(c[i], c[i + d])
        d //= 2
    return c


def _top_sorted(vals):
    k = PEER_TOPK
    lists = [_sort_desc(vals[g:g + k]) for g in range(0, len(vals), k)]
    while len(lists) > 1:
        nxt = [_merge_top(lists[i], lists[i + 1]) for i in range(0, len(lists) - 1, 2)]
        if len(lists) % 2:
            nxt.append(lists[-1])
        lists = nxt
    return lists[0]


_CAND_PAIRS = [(a, b) for a in range(PEER_TOPK) for b in range(PEER_TOPK) if (a + 1) * (b + 1) <= PEER_TOPK]


def _peer_select_kernel(q_ref, ks_ref, kh_ref, sh_ref, tau_ref, cn_ref, ss_scr, *, n_keys):
    tm = q_ref.shape[0]
    half = ks_ref.shape[2]
    for p in range(2):
        qp = q_ref[:, p * half:(p + 1) * half]
        ss_scr[p] = lax.dot_general(ks_ref[p], qp, _CONTRACT_LAST, preferred_element_type=F32)
        sh_ref[p] = lax.dot_general(kh_ref[p], qp, _CONTRACT_LAST, preferred_element_type=F32)

    def lane_tile(t, carry):
        off = pl.multiple_of(t * LANES, LANES)
        tops = []
        for p in range(2):
            vals = [ss_scr[p, v * SUBLANES:(v + 1) * SUBLANES, pl.ds(off, LANES)] for v in range(n_keys)]
            tops.append(_top_sorted(vals))
        cands = [tops[0][a] + tops[1][b] for a, b in _CAND_PAIRS]
        pad = [jnp.full_like(cands[0], -jnp.inf)] * (-len(cands) % PEER_TOPK)
        tau = _top_sorted(cands + pad)[PEER_TOPK - 1]
        top = tops[0][0] + tops[1][0]
        z = jnp.zeros_like(top)
        for c in cands:
            z = z + jnp.where(c >= tau, jnp.exp(c - top), 0.0)
        tau_ref[:, pl.ds(off, LANES)] = tau
        cn_ref[:, pl.ds(off, LANES)] = top + jnp.log(z)
        return carry

    lax.fori_loop(0, tm // LANES, lane_tile, 0)


def _peer_select(q, k_sort, k_head, n_keys):
    T, QW = q.shape
    rows = k_sort.shape[1]
    heads = rows // n_keys
    tm = _tile(T, 512)
    vmem = 2 * (_nbytes((tm, QW), BF16) + 2 * _nbytes(k_sort.shape, BF16) + _nbytes((2, rows, tm), F32)
                + 2 * _nbytes((heads, tm), F32)) + 4 * _nbytes((rows, tm), F32)
    return pl.pallas_call(
        functools.partial(_peer_select_kernel, n_keys=n_keys),
        grid=(T // tm,),
        in_specs=[pl.BlockSpec((tm, QW), lambda i: (i, 0)),
                  pl.BlockSpec(k_sort.shape, lambda i: (0, 0, 0)),
                  pl.BlockSpec(k_head.shape, lambda i: (0, 0, 0))],
        out_specs=[pl.BlockSpec((2, rows, tm), lambda i: (0, 0, i)),
                   pl.BlockSpec((heads, tm), lambda i: (0, i)),
                   pl.BlockSpec((heads, tm), lambda i: (0, i))],
        out_shape=[jax.ShapeDtypeStruct((2, rows, T), F32),
                   jax.ShapeDtypeStruct((heads, T), F32),
                   jax.ShapeDtypeStruct((heads, T), F32)],
        scratch_shapes=[pltpu.VMEM((2, rows, tm), F32)],
        compiler_params=_cparams(("parallel",), vmem),
        name="peer_select",
    )(q, k_sort, k_head)


def _peer_main_kernel(h_ref, u_ref, v_ref, sh_ref, tau_ref, cn_ref, o_ref, coef_scr, *, n_keys):
    e = pl.program_id(1)
    heads = tau_ref.shape[0]
    eb = u_ref.shape[0]

    @pl.when(e == 0)
    def _():
        o_ref[...] = jnp.zeros_like(o_ref)

    act_t = lax.dot_general(u_ref[...], h_ref[...], _CONTRACT_LAST, preferred_element_type=F32)
    for ii in range(eb // n_keys):
        i_key = e * (eb // n_keys) + ii
        w = None
        for h in range(heads):
            sm = sh_ref[0, pl.ds(h * n_keys + i_key, 1), :] + sh_ref[1, h * n_keys:(h + 1) * n_keys, :]
            term = jnp.where(sm >= tau_ref[h:h + 1, :], jnp.exp(sm - cn_ref[h:h + 1, :]), 0.0)
            w = term if w is None else w + term
        a = act_t[ii * n_keys:(ii + 1) * n_keys, :]
        gelu = 0.5 * a * (1.0 + lax.erf(a * (2.0 ** -0.5)))
        coef_scr[ii * n_keys:(ii + 1) * n_keys, :] = (w * gelu).astype(coef_scr.dtype)
    o_ref[...] += lax.dot_general(coef_scr[...], v_ref[...], _CONTRACT_FIRST, preferred_element_type=F32)


def _peer_main(h2, u_bf, v_bf, ss, tau, cn, n_keys):
    T, D = h2.shape
    NE = u_bf.shape[0]
    heads = tau.shape[0]
    rows = ss.shape[1]
    tm = _tile(T, 512)
    eb = _tile(NE, 512)
    assert eb % n_keys == 0
    vmem = 2 * (_nbytes((tm, D), BF16) + 2 * _nbytes((eb, D), BF16) + _nbytes((2, rows, tm), F32)
                + _nbytes((tm, D), F32)) + 8 * _nbytes((eb, tm), F32)
    return pl.pallas_call(
        functools.partial(_peer_main_kernel, n_keys=n_keys),
        grid=(T // tm, NE // eb),
        in_specs=[pl.BlockSpec((tm, D), lambda i, e: (i, 0)),
                  pl.BlockSpec((eb, D), lambda i, e: (e, 0)),
                  pl.BlockSpec((eb, D), lambda i, e: (e, 0)),
                  pl.BlockSpec((2, rows, tm), lambda i, e: (0, 0, i)),
                  pl.BlockSpec((heads, tm), lambda i, e: (0, i)),
                  pl.BlockSpec((heads, tm), lambda i, e: (0, i))],
        out_specs=pl.BlockSpec((tm, D), lambda i, e: (i, 0)),
        out_shape=jax.ShapeDtypeStruct((T, D), F32),
        scratch_shapes=[pltpu.VMEM((eb, tm), BF16)],
        compiler_params=_cparams(("parallel", "arbitrary"), vmem),
        name="peer_main",
    )(h2, u_bf, v_bf, ss, tau, cn)


def _residual_kernel(x_ref, p_ref, g_ref, o_ref):
    o_ref[0] = x_ref[0] + g_ref[0] * p_ref[0]


def _residual(x1, peer_out, mod3, r0, k_gate):
    B, S, D = x1.shape
    ts = _tile(S, 512)
    vmem = 2 * 3 * _nbytes((ts, D), F32) + _nbytes((ts, D), F32)
    return pl.pallas_call(
        _residual_kernel,
        grid=(B, S // ts),
        in_specs=[pl.BlockSpec((1, ts, D), lambda b, s: (b, s, 0)),
                  pl.BlockSpec((1, ts, D), lambda b, s: (b, s, 0)),
                  pl.BlockSpec((1, 1, D), lambda b, s: (r0 + b, 0, k_gate))],
        out_specs=pl.BlockSpec((1, ts, D), lambda b, s: (b, s, 0)),
        out_shape=jax.ShapeDtypeStruct((B, S, D), F32),
        compiler_params=_cparams(("parallel", "parallel"), vmem),
        name="residual",
    )(x1, peer_out, mod3)


def _rope_tables(seq_len, hd):
    rows = jnp.repeat(jnp.arange(seq_len // GRID_W), GRID_W).astype(F32)
    cols = jnp.tile(jnp.arange(GRID_W), seq_len // GRID_W).astype(F32)
    axis_dim = hd // 2
    inv_freq = ROPE_THETA ** (-jnp.arange(0, axis_dim, 2, dtype=F32) / axis_dim)
    ang_r = rows[:, None] * inv_freq
    ang_c = cols[:, None] * inv_freq
    cos_t = jnp.concatenate([jnp.cos(ang_r), jnp.cos(ang_r), jnp.cos(ang_c), jnp.cos(ang_c)], axis=-1)
    sin_t = jnp.concatenate([-jnp.sin(ang_r), jnp.sin(ang_r), -jnp.sin(ang_c), jnp.sin(ang_c)], axis=-1)
    return cos_t, sin_t


def _block_keys(keys):
    H, _, NK, C = keys.shape
    eye = jnp.eye(H, dtype=keys.dtype)
    k_sort = jnp.einsum('hpkc,hg->pkhgc', keys, eye).reshape(2, NK * H, H * C)
    k_head = jnp.einsum('hpkc,hg->phkgc', keys, eye).reshape(2, H * NK, H * C)
    return k_sort.astype(BF16), k_head.astype(BF16)


def _encoder_layer(x, r0, mod3, p):
    B, S, D = x.shape
    hd, attn_w, kv_w, lru_w = p['hd'], p['attn_w'], p['kv_w'], p['lru_w']
    cos_t, sin_t = _rope_tables(S, hd)

    h = _modnorm(x, mod3, r0, p['g_norm1'], 0, 1)
    qkv = _qkv_proj(h, p['w_in'], p['gq'], p['gk'], cos_t, sin_t, attn_w, kv_w, hd)
    proj2 = _proj2(h, p['w_in'], attn_w + 2 * kv_w, lru_w)
    attn = _attention(qkv, attn_w, kv_w, hd)
    hf = _lru(proj2, p['conv_w'], p['conv_b'], p['w_gate'][0], p['b_gate'][0], p['lam'][0], lru_w,
              reverse=False)
    lru = _lru(proj2, p['conv_w'], p['conv_b'], p['w_gate'][1], p['b_gate'][1], p['lam'][1], lru_w,
               reverse=True, hf=hf)
    merged = _merge(attn, lru, p['w_attn_o'], p['w_lru_o'], proj2, lru_w)
    x1 = _outproj(merged, p['w_out'], x, mod3, r0, 2)

    h2 = _modnorm(x1, mod3, r0, p['g_norm2'], 3, 4).reshape(B * S, D)
    q = _matmul(h2, p['peer_wq'], BF16)
    sh, tau, cn = _peer_select(q, p['k_sort'], p['k_head'], p['n_keys'])
    peer = _peer_main(h2, p['peer_u'], p['peer_v'], sh, tau, cn, p['n_keys'])
    return _residual(x1, peer.reshape(B, S, D), mod3, r0, 5)


def kernel(x_prompt, x_sample, c_prompt, c_sample, w_ada, b_ada, g_norm1, g_norm2, w_in, q_gain, k_gain,
           conv_w, conv_b, lru_lam, lru_wa, lru_ba, lru_wi, lru_bi, w_attn_o, w_lru_o, w_out,
           peer_wq, peer_keys, peer_u, peer_v):
    depth = w_ada.shape[0]
    D = x_prompt.shape[-1]
    hd = q_gain.shape[-1]
    attn_w = w_attn_o.shape[1]
    lru_w = w_lru_o.shape[1]
    kv_w = (w_in.shape[2] - attn_w - 2 * lru_w - 2 * D) // 2
    heads, _, n_keys, half = peer_keys.shape[1:]
    assert heads == SUBLANES, "selection layout puts one expert head per sublane"

    n_p, n_s = c_prompt.shape[0], c_sample.shape[0]
    rows = -(-(n_p + n_s) // SUBLANES) * SUBLANES
    c_rows = jnp.zeros((rows, D), F32).at[:n_p].set(c_prompt).at[n_p:n_p + n_s].set(c_sample)

    y_p, y_s = x_prompt, x_sample
    for l in range(depth):
        wq = peer_wq[l].reshape(D, heads, 2, half).transpose(0, 2, 1, 3).reshape(D, 2 * heads * half)
        k_sort, k_head = _block_keys(peer_keys[l])
        p = dict(
            hd=hd, attn_w=attn_w, kv_w=kv_w, lru_w=lru_w, n_keys=n_keys,
            g_norm1=g_norm1[l], g_norm2=g_norm2[l],
            w_in=w_in[l].astype(BF16),
            gq=(q_gain[l] * hd ** -0.5).reshape(1, hd), gk=k_gain[l].reshape(1, hd),
            conv_w=conv_w[l], conv_b=conv_b[l], lam=lru_lam[l],
            w_gate=jnp.concatenate([lru_wa[l], lru_wi[l]], axis=-1).astype(BF16),
            b_gate=jnp.stack([lru_ba[l], lru_bi[l]], axis=1),
            w_attn_o=w_attn_o[l].astype(BF16), w_lru_o=w_lru_o[l].astype(BF16), w_out=w_out[l].astype(BF16),
            peer_wq=wq.astype(BF16), k_sort=k_sort, k_head=k_head,
            peer_u=peer_u[l].astype(BF16), peer_v=peer_v[l].astype(BF16),
        )
        mod = _adaln(c_rows, w_ada[l], b_ada[l])
        mod3 = mod.reshape(rows, 1, N_MOD * D)
        y_p = _encoder_layer(y_p, 0, mod3, p)
        y_s = _encoder_layer(y_s, n_p, mod3, p)
    return (y_p, y_s)
```

```python
import functools
import math

import jax
import jax.numpy as jnp
from jax import lax
from jax.experimental import pallas as pl
from jax.experimental.pallas import tpu as pltpu

F32 = jnp.float32
BF16 = jnp.bfloat16

EPS = 1e-6
ROPE_THETA = 10000.0
GRID_W = 64
LRU_C = 8.0
CONV_LEFT = 2
PEER_TOPK = 16
N_MOD = 6

V7X_VMEM_BYTES = 64 * 1024 * 1024
V7X_VMEM_RESERVE = 8 * 1024 * 1024
LANES = 128
SUBLANES = 8

LRU_TIME_CHUNK = 1024
ATTN_Q_TILE = 256
ATTN_KV_TILE = 512

_CONTRACT_LAST = (((1,), (1,)), ((), ()))
_CONTRACT_FIRST = (((0,), (0,)), ((), ()))


def _cparams(semantics, vmem_bytes):
    limit = min(int(vmem_bytes), V7X_VMEM_BYTES - V7X_VMEM_RESERVE)
    return pltpu.CompilerParams(dimension_semantics=semantics, vmem_limit_bytes=limit)


def _tile(n, pref, *also):
    g = math.gcd(n, *also)
    for t in range(min(pref, g) // LANES * LANES, 0, -LANES):
        if g % t == 0:
            return t
    return g


def _nbytes(shape, dtype):
    return math.prod(shape) * jnp.dtype(dtype).itemsize


def _ada_kernel(c_ref, w_ref, b_ref, o_ref):
    c = c_ref[...]
    s = c * jax.nn.sigmoid(c)
    o_ref[...] = jnp.dot(s, w_ref[...], preferred_element_type=F32,
                         precision=lax.Precision.HIGHEST) + b_ref[...]


def _adaln(c_rows, w_ada, b_ada):
    R, D = c_rows.shape
    N = w_ada.shape[1]
    tn = _tile(N, 512)
    vmem = 2 * (_nbytes((R, D), F32) + _nbytes((D, tn), F32) + 2 * _nbytes((R, tn), F32)) \
        + 3 * _nbytes((D, tn), F32)
    return pl.pallas_call(
        _ada_kernel,
        grid=(N // tn,),
        in_specs=[pl.BlockSpec((R, D), lambda j: (0, 0)),
                  pl.BlockSpec((D, tn), lambda j: (0, j)),
                  pl.BlockSpec((1, tn), lambda j: (0, j))],
        out_specs=pl.BlockSpec((R, tn), lambda j: (0, j)),
        out_shape=jax.ShapeDtypeStruct((R, N), F32),
        compiler_params=_cparams(("parallel",), vmem),
        name="adaln",
    )(c_rows, w_ada, b_ada.reshape(1, N))


def _modnorm_kernel(x_ref, g_ref, sh_ref, sc_ref, o_ref):
    x = x_ref[0]
    y = x * lax.rsqrt(jnp.mean(x * x, axis=-1, keepdims=True) + EPS) * g_ref[...]
    o_ref[0] = (y * (1.0 + sc_ref[0]) + sh_ref[0]).astype(o_ref.dtype)


def _modnorm(x, mod3, r0, g, k_shift, k_scale):
    B, S, D = x.shape
    ts = _tile(S, 512)
    vmem = 2 * (_nbytes((ts, D), F32) + _nbytes((ts, D), BF16)) + 3 * _nbytes((ts, D), F32)
    return pl.pallas_call(
        _modnorm_kernel,
        grid=(B, S // ts),
        in_specs=[pl.BlockSpec((1, ts, D), lambda b, s: (b, s, 0)),
                  pl.BlockSpec((1, D), lambda b, s: (0, 0)),
                  pl.BlockSpec((1, 1, D), lambda b, s: (r0 + b, 0, k_shift)),
                  pl.BlockSpec((1, 1, D), lambda b, s: (r0 + b, 0, k_scale))],
        out_specs=pl.BlockSpec((1, ts, D), lambda b, s: (b, s, 0)),
        out_shape=jax.ShapeDtypeStruct((B, S, D), BF16),
        compiler_params=_cparams(("parallel", "parallel"), vmem),
        name="modnorm",
    )(x, g.reshape(1, D), mod3, mod3)


def _qkv_kernel(h_ref, w_ref, gq_ref, gk_ref, cos_ref, sin_ref, o_ref, *, n_q, n_qk, hd):
    j = pl.program_id(2)
    acc = jnp.dot(h_ref[0], w_ref[...], preferred_element_type=F32)
    tn = acc.shape[1]

    @pl.when(j < n_qk)
    def _():
        gain = jnp.where(j < n_q, gq_ref[...], gk_ref[...])
        cos = cos_ref[...]
        sin = sin_ref[...]
        lane = lax.broadcasted_iota(jnp.int32, (1, hd), 1)
        first = (lane % (hd // 2)) < (hd // 4)
        for hh in range(tn // hd):
            y = acc[:, hh * hd:(hh + 1) * hd]
            y = y * lax.rsqrt(jnp.mean(y * y, axis=-1, keepdims=True) + EPS) * gain
            sw = jnp.where(first, pltpu.roll(y, hd - hd // 4, 1), pltpu.roll(y, hd // 4, 1))
            o_ref[0, hh] = (y * cos + sw * sin).astype(o_ref.dtype)

    @pl.when(j >= n_qk)
    def _():
        for hh in range(tn // hd):
            o_ref[0, hh] = acc[:, hh * hd:(hh + 1) * hd].astype(o_ref.dtype)


def _qkv_proj(h, w_in_bf, gq, gk, cos_t, sin_t, attn_w, kv_w, hd):
    B, S, D = h.shape
    n_cols = attn_w + 2 * kv_w
    tn = _tile(attn_w, 512, kv_w)
    tm = _tile(S, 1024)
    vmem = 2 * (_nbytes((tm, D), BF16) + _nbytes((D, tn), BF16) + _nbytes((tm, tn), BF16)
                + 2 * _nbytes((tm, hd), F32)) + 4 * _nbytes((tm, tn), F32)
    kern = functools.partial(_qkv_kernel, n_q=attn_w // tn, n_qk=(attn_w + kv_w) // tn, hd=hd)
    return pl.pallas_call(
        kern,
        grid=(B, S // tm, n_cols // tn),
        in_specs=[pl.BlockSpec((1, tm, D), lambda b, i, j: (b, i, 0)),
                  pl.BlockSpec((D, tn), lambda b, i, j: (0, j)),
                  pl.BlockSpec((1, hd), lambda b, i, j: (0, 0)),
                  pl.BlockSpec((1, hd), lambda b, i, j: (0, 0)),
                  pl.BlockSpec((tm, hd), lambda b, i, j: (i, 0)),
                  pl.BlockSpec((tm, hd), lambda b, i, j: (i, 0))],
        out_specs=pl.BlockSpec((1, tn // hd, tm, hd), lambda b, i, j: (b, j, i, 0)),
        out_shape=jax.ShapeDtypeStruct((B, n_cols // hd, S, hd), BF16),
        compiler_params=_cparams(("parallel", "parallel", "arbitrary"), vmem),
        name="qkv_proj",
    )(h, w_in_bf, gq, gk, cos_t, sin_t)


def _proj2_kernel(h_ref, w_ref, o_ref, *, n_plain):
    j = pl.program_id(2)
    acc = jnp.dot(h_ref[0], w_ref[...], preferred_element_type=F32)

    @pl.when(j < n_plain)
    def _():
        o_ref[0] = acc

    @pl.when(j >= n_plain)
    def _():
        o_ref[0] = jax.nn.sigmoid(acc)


def _proj2(h, w_in_bf, col0, lru_w):
    B, S, D = h.shape
    n_cols = 2 * lru_w + 2 * D
    tn = _tile(2 * lru_w, 1024, 2 * D, col0)
    tm = _tile(S, 1024)
    j0 = col0 // tn
    vmem = 2 * (_nbytes((tm, D), BF16) + _nbytes((D, tn), BF16) + _nbytes((tm, tn), F32)) \
        + 4 * _nbytes((tm, tn), F32)
    return pl.pallas_call(
        functools.partial(_proj2_kernel, n_plain=2 * lru_w // tn),
        grid=(B, S // tm, n_cols // tn),
        in_specs=[pl.BlockSpec((1, tm, D), lambda b, i, j: (b, i, 0)),
                  pl.BlockSpec((D, tn), lambda b, i, j: (0, j0 + j))],
        out_specs=pl.BlockSpec((1, tm, tn), lambda b, i, j: (b, i, j)),
        out_shape=jax.ShapeDtypeStruct((B, S, n_cols), F32),
        compiler_params=_cparams(("parallel", "parallel", "arbitrary"), vmem),
        name="proj2",
    )(h, w_in_bf)


def _attn_kernel(q_ref, k_ref, v_ref, o_ref, m_scr, l_scr, acc_scr, *, tk):
    _, G, tq, hd = q_ref.shape
    rows = G * tq
    nk = k_ref.shape[2] // tk
    q = q_ref[0].reshape(rows, hd)
    m_scr[...] = jnp.full_like(m_scr, -jnp.inf)
    l_scr[...] = jnp.zeros_like(l_scr)
    acc_scr[...] = jnp.zeros_like(acc_scr)

    def body(c, carry):
        off = pl.multiple_of(c * tk, tk)
        k = k_ref[0, 0, pl.ds(off, tk), :]
        v = v_ref[0, 0, pl.ds(off, tk), :]
        s = lax.dot_general(q, k, _CONTRACT_LAST, preferred_element_type=F32)
        m_prev = m_scr[...]
        m_new = jnp.maximum(m_prev, jnp.max(s, axis=-1, keepdims=True))
        alpha = jnp.exp2(m_prev - m_new)
        p = jnp.exp2(s - pltpu.repeat(m_new, tk // LANES, axis=1))
        l_scr[...] = alpha * l_scr[...] + jnp.sum(p, axis=-1, keepdims=True)
        acc_scr[...] = alpha * acc_scr[...] + jnp.dot(p.astype(v.dtype), v, preferred_element_type=F32)
        m_scr[...] = m_new
        return carry

    lax.fori_loop(0, nk, body, 0)
    out = acc_scr[...] / l_scr[...]
    for g in range(G):
        o_ref[0, :, g * hd:(g + 1) * hd] = out[g * tq:(g + 1) * tq].astype(o_ref.dtype)


def _attention(qkv, attn_w, kv_w, hd):
    B, _, S, _ = qkv.shape
    assert hd == LANES, "running max / sum scratch is kept lane-replicated at head width"
    kvh = kv_w // hd
    G = attn_w // kv_w
    tq = _tile(S, ATTN_Q_TILE)
    tk = _tile(S, ATTN_KV_TILE)
    k0 = attn_w // hd
    v0 = (attn_w + kv_w) // hd
    rows = G * tq
    vmem = 2 * (2 * _nbytes((rows, hd), BF16) + 2 * _nbytes((S, hd), BF16)) + 3 * _nbytes((rows, hd), F32) \
        + 4 * _nbytes((rows, tk), F32)
    return pl.pallas_call(
        functools.partial(_attn_kernel, tk=tk),
        grid=(B, kvh, S // tq),
        in_specs=[pl.BlockSpec((1, G, tq, hd), lambda b, n, i: (b, n, i, 0)),
                  pl.BlockSpec((1, 1, S, hd), lambda b, n, i: (b, k0 + n, 0, 0)),
                  pl.BlockSpec((1, 1, S, hd), lambda b, n, i: (b, v0 + n, 0, 0))],
        out_specs=pl.BlockSpec((1, tq, G * hd), lambda b, n, i: (b, i, n)),
        out_shape=jax.ShapeDtypeStruct((B, S, attn_w), BF16),
        scratch_shapes=[pltpu.VMEM((rows, hd), F32)] * 3,
        compiler_params=_cparams(("parallel", "parallel", "arbitrary"), vmem),
        name="attention",
    )(qkv, qkv, qkv)


def _lru_kernel(*refs, reverse, n_chunks, conv_taps):
    if reverse:
        (cur_ref, prev_ref, next_ref, cw_ref, cb_ref, w_ref, b_ref, lam_ref, hf_ref, yl_ref,
         o_ref, ext_scr, a_scr, u_scr, carry_scr) = refs
    else:
        (cur_ref, prev_ref, next_ref, cw_ref, cb_ref, w_ref, b_ref, lam_ref,
         o_ref, ext_scr, a_scr, u_scr, carry_scr) = refs
    tc, wc = a_scr.shape
    bw = w_ref.shape[1]
    s = pl.program_id(2)
    chunk = (n_chunks - 1 - s) if reverse else s

    ext_scr[0:SUBLANES, :] = jnp.where(chunk > 0, prev_ref[0], 0.0)
    ext_scr[SUBLANES:SUBLANES + tc, :] = cur_ref[0]
    ext_scr[SUBLANES + tc:, :] = jnp.where(chunk < n_chunks - 1, next_ref[0], 0.0)
    xc = cb_ref[...]
    for j in range(conv_taps):
        r0 = SUBLANES - CONV_LEFT + j
        xc = xc + ext_scr[r0:r0 + tc, :] * cw_ref[j:j + 1, :]

    lam = -lam_ref[...]
    softplus_neg_lam = jnp.maximum(lam, 0.0) + jnp.log1p(jnp.exp(-jnp.abs(lam)))
    for n in range(wc // bw):
        cols = slice(n * bw, (n + 1) * bw)
        xb = xc[:, cols]
        z = jnp.dot(xb.astype(BF16), w_ref[n], preferred_element_type=F32)
        r = jax.nn.sigmoid(z[:, :bw] + b_ref[0:1, cols])
        i = jax.nn.sigmoid(z[:, bw:] + b_ref[1:2, cols])
        log_a = -LRU_C * r * softplus_neg_lam[:, cols]
        a = jnp.exp(log_a)
        mult = jnp.sqrt(jnp.maximum(-jnp.tanh(log_a) * (a * a + 1.0), 0.0))
        a_scr[:, cols] = a
        u_scr[:, cols] = mult * (i * xb)

    @pl.when(s == 0)
    def _():
        carry_scr[...] = jnp.zeros_like(carry_scr)

    row = lax.broadcasted_iota(jnp.int32, (SUBLANES, wc), 0)
    n_tiles = tc // SUBLANES

    def tile_body(t, h_in):
        tile = (n_tiles - 1 - t) if reverse else t
        r0 = pl.multiple_of(tile * SUBLANES, SUBLANES)
        a = a_scr[pl.ds(r0, SUBLANES), :]
        u = u_scr[pl.ds(r0, SUBLANES), :]
        for d in (1, 2, 4):
            if reverse:
                valid = row < SUBLANES - d
                shift = SUBLANES - d
            else:
                valid = row >= d
                shift = d
            u = u + a * jnp.where(valid, pltpu.roll(u, shift, 0), 0.0)
            a = a * jnp.where(valid, pltpu.roll(a, shift, 0), 1.0)
        h = u + a * h_in
        u_scr[pl.ds(r0, SUBLANES), :] = h
        edge = h[0:1, :] if reverse else h[SUBLANES - 1:SUBLANES, :]
        return jnp.broadcast_to(edge, (SUBLANES, wc))

    carry_scr[...] = lax.fori_loop(0, n_tiles, tile_body, carry_scr[...], unroll=2)

    if reverse:
        y = yl_ref[0]
        gelu = 0.5 * y * (1.0 + lax.erf(y * (2.0 ** -0.5)))
        o_ref[0] = ((hf_ref[0] + u_scr[...]) * gelu).astype(o_ref.dtype)
    else:
        o_ref[0] = u_scr[...]


def _lru(proj2, conv_w, conv_b, w_gate, b_gate, lam, lru_w, *, reverse, hf=None):
    B, S, _ = proj2.shape
    nb, bw, _ = w_gate.shape
    taps = conv_w.shape[0]
    wc = _tile(lru_w, 512)
    tc = _tile(S, LRU_TIME_CHUNK)
    n_chunks = S // tc
    n_wc = lru_w // wc
    halo_blocks = S // SUBLANES
    per_chunk = tc // SUBLANES

    def t_idx(s):
        return (n_chunks - 1 - s) if reverse else s

    in_specs = [
        pl.BlockSpec((1, tc, wc), lambda b, c, s: (b, t_idx(s), c)),
        pl.BlockSpec((1, SUBLANES, wc), lambda b, c, s: (b, jnp.maximum(t_idx(s) * per_chunk - 1, 0), c)),
        pl.BlockSpec((1, SUBLANES, wc),
                     lambda b, c, s: (b, jnp.minimum((t_idx(s) + 1) * per_chunk, halo_blocks - 1), c)),
        pl.BlockSpec((taps, wc), lambda b, c, s: (0, c)),
        pl.BlockSpec((1, wc), lambda b, c, s: (0, c)),
        pl.BlockSpec((wc // bw, bw, 2 * bw), lambda b, c, s: (c, 0, 0)),
        pl.BlockSpec((2, wc), lambda b, c, s: (0, c)),
        pl.BlockSpec((1, wc), lambda b, c, s: (0, c)),
    ]
    args = [proj2, proj2, proj2, conv_w, conv_b.reshape(1, lru_w), w_gate, b_gate, lam.reshape(1, lru_w)]
    if reverse:
        in_specs += [pl.BlockSpec((1, tc, wc), lambda b, c, s: (b, t_idx(s), c)),
                     pl.BlockSpec((1, tc, wc), lambda b, c, s: (b, t_idx(s), n_wc + c))]
        args += [hf, proj2]
        out_dtype = BF16
    else:
        out_dtype = F32
    blk = _nbytes((tc, wc), F32)
    vmem = 2 * (4 * blk) + 3 * blk + 10 * blk
    return pl.pallas_call(
        functools.partial(_lru_kernel, reverse=reverse, n_chunks=n_chunks, conv_taps=taps),
        grid=(B, n_wc, n_chunks),
        in_specs=in_specs,
        out_specs=pl.BlockSpec((1, tc, wc), lambda b, c, s: (b, t_idx(s), c)),
        out_shape=jax.ShapeDtypeStruct((B, S, lru_w), out_dtype),
        scratch_shapes=[pltpu.VMEM((tc + 2 * SUBLANES, wc), F32),
                        pltpu.VMEM((tc, wc), F32),
                        pltpu.VMEM((tc, wc), F32),
                        pltpu.VMEM((SUBLANES, wc), F32)],
        compiler_params=_cparams(("parallel", "parallel", "arbitrary"), vmem),
        name="lru_bwd" if reverse else "lru_fwd",
    )(*args)


def _merge_kernel(attn_ref, lru_ref, wa_ref, wl_ref, ga_ref, gl_ref, o_ref):
    a = jnp.dot(attn_ref[0], wa_ref[...], preferred_element_type=F32)
    l = jnp.dot(lru_ref[0], wl_ref[...], preferred_element_type=F32)
    o_ref[0] = (ga_ref[0] * a + gl_ref[0] * l).astype(o_ref.dtype)


def _merge(attn, lru, w_attn_o_bf, w_lru_o_bf, proj2, lru_w):
    B, S, attn_w = attn.shape
    D = w_attn_o_bf.shape[1]
    tm = _tile(S, 1024)
    tn = _tile(D, 512, 2 * lru_w)
    ga0 = 2 * lru_w // tn
    gl0 = (2 * lru_w + D) // tn
    vmem = 2 * (_nbytes((tm, attn_w), BF16) + _nbytes((tm, lru_w), BF16) + _nbytes((attn_w, tn), BF16)
                + _nbytes((lru_w, tn), BF16) + 2 * _nbytes((tm, tn), F32) + _nbytes((tm, tn), BF16)) \
        + 3 * _nbytes((tm, tn), F32)
    return pl.pallas_call(
        _merge_kernel,
        grid=(B, S // tm, D // tn),
        in_specs=[pl.BlockSpec((1, tm, attn_w), lambda b, i, j: (b, i, 0)),
                  pl.BlockSpec((1, tm, lru_w), lambda b, i, j: (b, i, 0)),
                  pl.BlockSpec((attn_w, tn), lambda b, i, j: (0, j)),
                  pl.BlockSpec((lru_w, tn), lambda b, i, j: (0, j)),
                  pl.BlockSpec((1, tm, tn), lambda b, i, j: (b, i, ga0 + j)),
                  pl.BlockSpec((1, tm, tn), lambda b, i, j: (b, i, gl0 + j))],
        out_specs=pl.BlockSpec((1, tm, tn), lambda b, i, j: (b, i, j)),
        out_shape=jax.ShapeDtypeStruct((B, S, D), BF16),
        compiler_params=_cparams(("parallel", "parallel", "arbitrary"), vmem),
        name="merge",
    )(attn, lru, w_attn_o_bf, w_lru_o_bf, proj2, proj2)


def _outproj_kernel(m_ref, w_ref, x_ref, g_ref, o_ref):
    o_ref[0] = x_ref[0] + g_ref[0] * jnp.dot(m_ref[0], w_ref[...], preferred_element_type=F32)


def _outproj(merged, w_out_bf, x, mod3, r0, k_gate):
    B, S, D = x.shape
    tm = _tile(S, 1024)
    tn = _tile(D, 1024)
    gpb = D // tn
    vmem = 2 * (_nbytes((tm, D), BF16) + _nbytes((D, tn), BF16) + 2 * _nbytes((tm, tn), F32)) \
        + 2 * _nbytes((tm, tn), F32)
    return pl.pallas_call(
        _outproj_kernel,
        grid=(B, S // tm, D // tn),
        in_specs=[pl.BlockSpec((1, tm, D), lambda b, i, j: (b, i, 0)),
                  pl.BlockSpec((D, tn), lambda b, i, j: (0, j)),
                  pl.BlockSpec((1, tm, tn), lambda b, i, j: (b, i, j)),
                  pl.BlockSpec((1, 1, tn), lambda b, i, j: (r0 + b, 0, k_gate * gpb + j))],
        out_specs=pl.BlockSpec((1, tm, tn), lambda b, i, j: (b, i, j)),
        out_shape=jax.ShapeDtypeStruct((B, S, D), F32),
        compiler_params=_cparams(("parallel", "parallel", "arbitrary"), vmem),
        name="outproj",
    )(merged, w_out_bf, x, mod3)


def _matmul_kernel(a_ref, w_ref, o_ref):
    o_ref[...] = jnp.dot(a_ref[...], w_ref[...], preferred_element_type=F32).astype(o_ref.dtype)


def _matmul(a, w, out_dtype):
    M, K = a.shape
    N = w.shape[1]
    tm = _tile(M, 1024)
    tn = _tile(N, 1024)
    vmem = 2 * (_nbytes((tm, K), a.dtype) + _nbytes((K, tn), w.dtype) + _nbytes((tm, tn), out_dtype)) \
        + 2 * _nbytes((tm, tn), F32)
    return pl.pallas_call(
        _matmul_kernel,
        grid=(M // tm, N // tn),
        in_specs=[pl.BlockSpec((tm, K), lambda i, j: (i, 0)),
                  pl.BlockSpec((K, tn), lambda i, j: (0, j))],
        out_specs=pl.BlockSpec((tm, tn), lambda i, j: (i, j)),
        out_shape=jax.ShapeDtypeStruct((M, N), out_dtype),
        compiler_params=_cparams(("parallel", "arbitrary"), vmem),
        name="peer_query",
    )(a, w)


def _batcher_pairs(n):
    pairs = []

    def merge(lo, hi, r):
        step = r * 2
        if step < hi - lo:
            merge(lo, hi, step)
            merge(lo + r, hi, step)
            pairs.extend((i, i + r) for i in range(lo + r, hi - r, step))
        else:
            pairs.append((lo, lo + r))

    def sort(lo, hi):
        if hi - lo >= 1:
            mid = lo + (hi - lo) // 2
            sort(lo, mid)
            sort(mid + 1, hi)
            merge(lo, hi, 1)

    sort(0, n - 1)
    return pairs


_SORT_PAIRS = _batcher_pairs(PEER_TOPK)


def _sort_desc(v):
    v = list(v)
    for i, j in _SORT_PAIRS:
        v[i], v[j] = jnp.maximum(v[i], v[j]), jnp.minimum(v[i], v[j])
    return v


def _merge_top(a, b):
    k = len(a)
    c = [jnp.maximum(a[i], b[k - 1 - i]) for i in range(k)]
    d = k // 2
    while d >= 1:
        for i in range(k):
            if not i & d:
                c[i], c[i + d] = jnp.maximum(c[i], c[i + d]), jnp.minimum(c[i], c[i + d])
        d //= 2
    return c


def _top_sorted(vals):
    k = PEER_TOPK
    lists = [_sort_desc(vals[g:g + k]) for g in range(0, len(vals), k)]
    while len(lists) > 1:
        nxt = [_merge_top(lists[i], lists[i + 1]) for i in range(0, len(lists) - 1, 2)]
        if len(lists) % 2:
            nxt.append(lists[-1])
        lists = nxt
    return lists[0]


_CAND_PAIRS = [(a, b) for a in range(PEER_TOPK) for b in range(PEER_TOPK) if (a + 1) * (b + 1) <= PEER_TOPK]


def _peer_select_kernel(q_ref, ks_ref, kh_ref, s0_ref, s1_ref, tau_ref, cn_ref, s1_scr, *, n_keys):
    half = ks_ref.shape[2]
    q0 = q_ref[:, :half]
    q1 = q_ref[:, half:]
    s0_ref[...] = lax.dot_general(ks_ref[0], q0, _CONTRACT_LAST, preferred_element_type=F32)
    s1_scr[...] = lax.dot_general(ks_ref[1], q1, _CONTRACT_LAST, preferred_element_type=F32)
    s1_ref[...] = lax.dot_general(kh_ref[...], q1, _CONTRACT_LAST, preferred_element_type=F32)

    def lane_tile(t, carry):
        lanes = pl.ds(pl.multiple_of(t * LANES, LANES), LANES)
        tops = []
        for s_ref in (s0_ref, s1_scr):
            vals = [s_ref[v * SUBLANES:(v + 1) * SUBLANES, lanes] for v in range(n_keys)]
            tops.append(_top_sorted(vals))
        cands = [tops[0][a] + tops[1][b] for a, b in _CAND_PAIRS]
        pad = [jnp.full_like(cands[0], -jnp.inf)] * (-len(cands) % PEER_TOPK)
        tau = _top_sorted(cands + pad)[PEER_TOPK - 1]
        top = tops[0][0] + tops[1][0]
        z = jnp.zeros_like(top)
        for c in cands:
            z = z + jnp.where(c >= tau, jnp.exp(c - top), 0.0)
        tau_ref[:, lanes] = tau
        cn_ref[:, lanes] = top + jnp.log(z)
        return carry

    lax.fori_loop(0, q_ref.shape[0] // LANES, lane_tile, 0)


def _peer_select(q, k_sort, k_head, n_keys):
    T, QW = q.shape
    rows = k_sort.shape[1]
    heads = rows // n_keys
    tm = _tile(T, 512)
    vmem = 2 * (_nbytes((tm, QW), BF16) + _nbytes(k_sort.shape, BF16) + _nbytes(k_head.shape, BF16)
                + 2 * _nbytes((rows, tm), F32) + 2 * _nbytes((heads, tm), F32)) + 4 * _nbytes((rows, tm), F32)
    return pl.pallas_call(
        functools.partial(_peer_select_kernel, n_keys=n_keys),
        grid=(T // tm,),
        in_specs=[pl.BlockSpec((tm, QW), lambda i: (i, 0)),
                  pl.BlockSpec(k_sort.shape, lambda i: (0, 0, 0)),
                  pl.BlockSpec(k_head.shape, lambda i: (0, 0))],
        out_specs=[pl.BlockSpec((rows, tm), lambda i: (0, i)),
                   pl.BlockSpec((rows, tm), lambda i: (0, i)),
                   pl.BlockSpec((heads, tm), lambda i: (0, i)),
                   pl.BlockSpec((heads, tm), lambda i: (0, i))],
        out_shape=[jax.ShapeDtypeStruct((rows, T), F32),
                   jax.ShapeDtypeStruct((rows, T), F32),
                   jax.ShapeDtypeStruct((heads, T), F32),
                   jax.ShapeDtypeStruct((heads, T), F32)],
        scratch_shapes=[pltpu.VMEM((rows, tm), F32)],
        compiler_params=_cparams(("parallel",), vmem),
        name="peer_select",
    )(q, k_sort, k_head)


def _peer_main_kernel(h_ref, u_ref, v_ref, s0_ref, s1_ref, tau_ref, cn_ref, o_ref, coef_scr, *, n_keys):
    e = pl.program_id(1)
    heads = tau_ref.shape[0]
    eb = u_ref.shape[0]

    @pl.when(e == 0)
    def _():
        o_ref[...] = jnp.zeros_like(o_ref)

    act_t = lax.dot_general(u_ref[...], h_ref[...], _CONTRACT_LAST, preferred_element_type=F32)
    for ii in range(eb // n_keys):
        w = None
        for h in range(heads):
            sm = s0_ref[ii * heads + h:ii * heads + h + 1, :] + s1_ref[h * n_keys:(h + 1) * n_keys, :]
            term = jnp.where(sm >= tau_ref[h:h + 1, :], jnp.exp(sm - cn_ref[h:h + 1, :]), 0.0)
            w = term if w is None else w + term
        a = act_t[ii * n_keys:(ii + 1) * n_keys, :]
        gelu = 0.5 * a * (1.0 + lax.erf(a * (2.0 ** -0.5)))
        coef_scr[ii * n_keys:(ii + 1) * n_keys, :] = (w * gelu).astype(coef_scr.dtype)
    o_ref[...] += lax.dot_general(coef_scr[...], v_ref[...], _CONTRACT_FIRST, preferred_element_type=F32)


def _peer_main(h2, u_bf, v_bf, s0, s1, tau, cn, n_keys):
    T, D = h2.shape
    NE = u_bf.shape[0]
    heads = tau.shape[0]
    rows = s1.shape[0]
    tm = _tile(T, 512)
    eb = _tile(NE, 512)
    assert eb % n_keys == 0
    s0_rows = eb // n_keys * heads
    vmem = 2 * (_nbytes((tm, D), BF16) + 2 * _nbytes((eb, D), BF16) + _nbytes((rows, tm), F32)
                + _nbytes((tm, D), F32)) + 8 * _nbytes((eb, tm), F32)
    return pl.pallas_call(
        functools.partial(_peer_main_kernel, n_keys=n_keys),
        grid=(T // tm, NE // eb),
        in_specs=[pl.BlockSpec((tm, D), lambda i, e: (i, 0)),
                  pl.BlockSpec((eb, D), lambda i, e: (e, 0)),
                  pl.BlockSpec((eb, D), lambda i, e: (e, 0)),
                  pl.BlockSpec((s0_rows, tm), lambda i, e: (e, i)),
                  pl.BlockSpec((rows, tm), lambda i, e: (0, i)),
                  pl.BlockSpec((heads, tm), lambda i, e: (0, i)),
                  pl.BlockSpec((heads, tm), lambda i, e: (0, i))],
        out_specs=pl.BlockSpec((tm, D), lambda i, e: (i, 0)),
        out_shape=jax.ShapeDtypeStruct((T, D), F32),
        scratch_shapes=[pltpu.VMEM((eb, tm), BF16)],
        compiler_params=_cparams(("parallel", "arbitrary"), vmem),
        name="peer_main",
    )(h2, u_bf, v_bf, s0, s1, tau, cn)


def _residual_kernel(x_ref, p_ref, g_ref, o_ref):
    o_ref[0] = x_ref[0] + g_ref[0] * p_ref[0]


def _residual(x1, peer_out, mod3, r0, k_gate):
    B, S, D = x1.shape
    ts = _tile(S, 512)
    vmem = 2 * 3 * _nbytes((ts, D), F32) + _nbytes((ts, D), F32)
    return pl.pallas_call(
        _residual_kernel,
        grid=(B, S // ts),
        in_specs=[pl.BlockSpec((1, ts, D), lambda b, s: (b, s, 0)),
                  pl.BlockSpec((1, ts, D), lambda b, s: (b, s, 0)),
                  pl.BlockSpec((1, 1, D), lambda b, s: (r0 + b, 0, k_gate))],
        out_specs=pl.BlockSpec((1, ts, D), lambda b, s: (b, s, 0)),
        out_shape=jax.ShapeDtypeStruct((B, S, D), F32),
        compiler_params=_cparams(("parallel", "parallel"), vmem),
        name="residual",
    )(x1, peer_out, mod3)


def _rope_tables(seq_len, hd):
    rows = jnp.repeat(jnp.arange(seq_len // GRID_W), GRID_W).astype(F32)
    cols = jnp.tile(jnp.arange(GRID_W), seq_len // GRID_W).astype(F32)
    axis_dim = hd // 2
    inv_freq = ROPE_THETA ** (-jnp.arange(0, axis_dim, 2, dtype=F32) / axis_dim)
    ang_r = rows[:, None] * inv_freq
    ang_c = cols[:, None] * inv_freq
    cos_t = jnp.concatenate([jnp.cos(ang_r), jnp.cos(ang_r), jnp.cos(ang_c), jnp.cos(ang_c)], axis=-1)
    sin_t = jnp.concatenate([-jnp.sin(ang_r), jnp.sin(ang_r), -jnp.sin(ang_c), jnp.sin(ang_c)], axis=-1)
    return cos_t, sin_t


def _block_keys(keys):
    H, _, NK, C = keys.shape
    eye = jnp.eye(H, dtype=keys.dtype)
    k_sort = jnp.einsum('hpkc,hg->pkhgc', keys, eye).reshape(2, NK * H, H * C)
    k_head = jnp.einsum('hkc,hg->hkgc', keys[:, 1], eye).reshape(H * NK, H * C)
    return k_sort.astype(BF16), k_head.astype(BF16)


def _encoder_layer(x, r0, mod3, p):
    B, S, D = x.shape
    hd, attn_w, kv_w, lru_w = p['hd'], p['attn_w'], p['kv_w'], p['lru_w']
    cos_t, sin_t = _rope_tables(S, hd)

    h = _modnorm(x, mod3, r0, p['g_norm1'], 0, 1)
    qkv = _qkv_proj(h, p['w_in'], p['gq'], p['gk'], cos_t, sin_t, attn_w, kv_w, hd)
    proj2 = _proj2(h, p['w_in'], attn_w + 2 * kv_w, lru_w)
    attn = _attention(qkv, attn_w, kv_w, hd)
    hf = _lru(proj2, p['conv_w'], p['conv_b'], p['w_gate'][0], p['b_gate'][0], p['lam'][0], lru_w,
              reverse=False)
    lru = _lru(proj2, p['conv_w'], p['conv_b'], p['w_gate'][1], p['b_gate'][1], p['lam'][1], lru_w,
               reverse=True, hf=hf)
    merged = _merge(attn, lru, p['w_attn_o'], p['w_lru_o'], proj2, lru_w)
    x1 = _outproj(merged, p['w_out'], x, mod3, r0, 2)

    h2 = _modnorm(x1, mod3, r0, p['g_norm2'], 3, 4).reshape(B * S, D)
    q = _matmul(h2, p['peer_wq'], BF16)
    s0, s1, tau, cn = _peer_select(q, p['k_sort'], p['k_head'], p['n_keys'])
    peer = _peer_main(h2, p['peer_u'], p['peer_v'], s0, s1, tau, cn, p['n_keys'])
    return _residual(x1, peer.reshape(B, S, D), mod3, r0, 5)


def kernel(x_prompt, x_sample, c_prompt, c_sample, w_ada, b_ada, g_norm1, g_norm2, w_in, q_gain, k_gain,
           conv_w, conv_b, lru_lam, lru_wa, lru_ba, lru_wi, lru_bi, w_attn_o, w_lru_o, w_out,
           peer_wq, peer_keys, peer_u, peer_v):
    depth = w_ada.shape[0]
    D = x_prompt.shape[-1]
    hd = q_gain.shape[-1]
    attn_w = w_attn_o.shape[1]
    lru_w = w_lru_o.shape[1]
    kv_w = (w_in.shape[2] - attn_w - 2 * lru_w - 2 * D) // 2
    heads, _, n_keys, half = peer_keys.shape[1:]
    assert heads == SUBLANES, "selection layout puts one expert head per sublane"

    n_p, n_s = c_prompt.shape[0], c_sample.shape[0]
    rows = -(-(n_p + n_s) // SUBLANES) * SUBLANES
    c_rows = jnp.zeros((rows, D), F32).at[:n_p].set(c_prompt).at[n_p:n_p + n_s].set(c_sample)

    y_p, y_s = x_prompt, x_sample
    for l in range(depth):
        wq = peer_wq[l].reshape(D, heads, 2, half).transpose(0, 2, 1, 3).reshape(D, 2 * heads * half)
        k_sort, k_head = _block_keys(peer_keys[l])
        p = dict(
            hd=hd, attn_w=attn_w, kv_w=kv_w, lru_w=lru_w, n_keys=n_keys,
            g_norm1=g_norm1[l], g_norm2=g_norm2[l],
            w_in=w_in[l].astype(BF16),
            gq=(q_gain[l] * (hd ** -0.5 * math.log2(math.e))).reshape(1, hd), gk=k_gain[l].reshape(1, hd),
            conv_w=conv_w[l], conv_b=conv_b[l], lam=lru_lam[l],
            w_gate=jnp.concatenate([lru_wa[l], lru_wi[l]], axis=-1).astype(BF16),
            b_gate=jnp.stack([lru_ba[l], lru_bi[l]], axis=1),
            w_attn_o=w_attn_o[l].astype(BF16), w_lru_o=w_lru_o[l].astype(BF16), w_out=w_out[l].astype(BF16),
            peer_wq=wq.astype(BF16), k_sort=k_sort, k_head=k_head,
            peer_u=peer_u[l].astype(BF16), peer_v=peer_v[l].astype(BF16),
        )
        mod = _adaln(c_rows, w_ada[l], b_ada[l])
        mod3 = mod.reshape(rows, 1, N_MOD * D)
        y_p = _encoder_layer(y_p, 0, mod3, p)
        y_s = _encoder_layer(y_s, n_p, mod3, p)
    return (y_p, y_s)
```

```python
import functools
import math

import jax
import jax.numpy as jnp
from jax import lax
from jax.experimental import pallas as pl
from jax.experimental.pallas import tpu as pltpu

F32 = jnp.float32
BF16 = jnp.bfloat16

EPS = 1e-6
ROPE_THETA = 10000.0
GRID_W = 64
LRU_C = 8.0
CONV_LEFT = 2
PEER_TOPK = 16
N_MOD = 6

V7X_VMEM_BYTES = 64 * 1024 * 1024
V7X_VMEM_RESERVE = 8 * 1024 * 1024
LANES = 128
SUBLANES = 8

LRU_TIME_CHUNK = 1024
ATTN_Q_TILE = 256
ATTN_KV_TILE = 512

_CONTRACT_LAST = (((1,), (1,)), ((), ()))
_CONTRACT_FIRST = (((0,), (0,)), ((), ()))


def _cparams(semantics, vmem_bytes):
    limit = min(int(vmem_bytes), V7X_VMEM_BYTES - V7X_VMEM_RESERVE)
    return pltpu.CompilerParams(dimension_semantics=semantics, vmem_limit_bytes=limit)


def _tile(n, pref, *also):
    g = math.gcd(n, *also)
    for t in range(min(pref, g) // LANES * LANES, 0, -LANES):
        if g % t == 0:
            return t
    return g


def _nbytes(shape, dtype):
    return math.prod(shape) * jnp.dtype(dtype).itemsize


def _ada_kernel(c_ref, w_ref, b_ref, o_ref):
    c = c_ref[...]
    s = c * jax.nn.sigmoid(c)
    o_ref[...] = jnp.dot(s, w_ref[...], preferred_element_type=F32,
                         precision=lax.Precision.HIGHEST) + b_ref[...]


def _adaln(c_rows, w_ada, b_ada):
    R, D = c_rows.shape
    N = w_ada.shape[1]
    tn = _tile(N, 512)
    vmem = 2 * (_nbytes((R, D), F32) + _nbytes((D, tn), F32) + 2 * _nbytes((R, tn), F32)) \
        + 3 * _nbytes((D, tn), F32)
    return pl.pallas_call(
        _ada_kernel,
        grid=(N // tn,),
        in_specs=[pl.BlockSpec((R, D), lambda j: (0, 0)),
                  pl.BlockSpec((D, tn), lambda j: (0, j)),
                  pl.BlockSpec((1, tn), lambda j: (0, j))],
        out_specs=pl.BlockSpec((R, tn), lambda j: (0, j)),
        out_shape=jax.ShapeDtypeStruct((R, N), F32),
        compiler_params=_cparams(("parallel",), vmem),
        name="adaln",
    )(c_rows, w_ada, b_ada.reshape(1, N))


def _modnorm_kernel(x_ref, g_ref, sh_ref, sc_ref, o_ref):
    x = x_ref[0]
    y = x * lax.rsqrt(jnp.mean(x * x, axis=-1, keepdims=True) + EPS) * g_ref[...]
    o_ref[0] = (y * (1.0 + sc_ref[0]) + sh_ref[0]).astype(o_ref.dtype)


def _modnorm(x, mod3, r0, g, k_shift, k_scale):
    B, S, D = x.shape
    ts = _tile(S, 512)
    vmem = 2 * (_nbytes((ts, D), F32) + _nbytes((ts, D), BF16)) + 3 * _nbytes((ts, D), F32)
    return pl.pallas_call(
        _modnorm_kernel,
        grid=(B, S // ts),
        in_specs=[pl.BlockSpec((1, ts, D), lambda b, s: (b, s, 0)),
                  pl.BlockSpec((1, D), lambda b, s: (0, 0)),
                  pl.BlockSpec((1, 1, D), lambda b, s: (r0 + b, 0, k_shift)),
                  pl.BlockSpec((1, 1, D), lambda b, s: (r0 + b, 0, k_scale))],
        out_specs=pl.BlockSpec((1, ts, D), lambda b, s: (b, s, 0)),
        out_shape=jax.ShapeDtypeStruct((B, S, D), BF16),
        compiler_params=_cparams(("parallel", "parallel"), vmem),
        name="modnorm",
    )(x, g.reshape(1, D), mod3, mod3)


def _qkv_kernel(h_ref, w_ref, gq_ref, gk_ref, cos_ref, sin_ref, o_ref, *, n_q, n_qk, hd):
    j = pl.program_id(2)
    acc = jnp.dot(h_ref[0], w_ref[...], preferred_element_type=F32)
    tn = acc.shape[1]

    @pl.when(j < n_qk)
    def _():
        gain = jnp.where(j < n_q, gq_ref[...], gk_ref[...])
        cos = cos_ref[...]
        sin = sin_ref[...]
        lane = lax.broadcasted_iota(jnp.int32, (1, hd), 1)
        first = (lane % (hd // 2)) < (hd // 4)
        for hh in range(tn // hd):
            y = acc[:, hh * hd:(hh + 1) * hd]
            y = y * lax.rsqrt(jnp.mean(y * y, axis=-1, keepdims=True) + EPS) * gain
            sw = jnp.where(first, pltpu.roll(y, hd - hd // 4, 1), pltpu.roll(y, hd // 4, 1))
            o_ref[0, hh] = (y * cos + sw * sin).astype(o_ref.dtype)

    @pl.when(j >= n_qk)
    def _():
        for hh in range(tn // hd):
            o_ref[0, hh] = acc[:, hh * hd:(hh + 1) * hd].astype(o_ref.dtype)


def _qkv_proj(h, w_in_bf, gq, gk, cos_t, sin_t, attn_w, kv_w, hd):
    B, S, D = h.shape
    n_cols = attn_w + 2 * kv_w
    tn = _tile(attn_w, 512, kv_w)
    tm = _tile(S, 1024)
    vmem = 2 * (_nbytes((tm, D), BF16) + _nbytes((D, tn), BF16) + _nbytes((tm, tn), BF16)
                + 2 * _nbytes((tm, hd), F32)) + 4 * _nbytes((tm, tn), F32)
    kern = functools.partial(_qkv_kernel, n_q=attn_w // tn, n_qk=(attn_w + kv_w) // tn, hd=hd)
    return pl.pallas_call(
        kern,
        grid=(B, S // tm, n_cols // tn),
        in_specs=[pl.BlockSpec((1, tm, D), lambda b, i, j: (b, i, 0)),
                  pl.BlockSpec((D, tn), lambda b, i, j: (0, j)),
                  pl.BlockSpec((1, hd), lambda b, i, j: (0, 0)),
                  pl.BlockSpec((1, hd), lambda b, i, j: (0, 0)),
                  pl.BlockSpec((tm, hd), lambda b, i, j: (i, 0)),
                  pl.BlockSpec((tm, hd), lambda b, i, j: (i, 0))],
        out_specs=pl.BlockSpec((1, tn // hd, tm, hd), lambda b, i, j: (b, j, i, 0)),
        out_shape=jax.ShapeDtypeStruct((B, n_cols // hd, S, hd), BF16),
        compiler_params=_cparams(("parallel", "parallel", "arbitrary"), vmem),
        name="qkv_proj",
    )(h, w_in_bf, gq, gk, cos_t, sin_t)


def _proj2_kernel(h_ref, w_ref, o_ref, *, n_plain):
    j = pl.program_id(2)
    acc = jnp.dot(h_ref[0], w_ref[...], preferred_element_type=F32)

    @pl.when(j < n_plain)
    def _():
        o_ref[0] = acc

    @pl.when(j >= n_plain)
    def _():
        o_ref[0] = jax.nn.sigmoid(acc)


def _proj2(h, w_in_bf, col0, lru_w):
    B, S, D = h.shape
    n_cols = 2 * lru_w + 2 * D
    tn = _tile(2 * lru_w, 1024, 2 * D, col0)
    tm = _tile(S, 1024)
    j0 = col0 // tn
    vmem = 2 * (_nbytes((tm, D), BF16) + _nbytes((D, tn), BF16) + _nbytes((tm, tn), F32)) \
        + 4 * _nbytes((tm, tn), F32)
    return pl.pallas_call(
        functools.partial(_proj2_kernel, n_plain=2 * lru_w // tn),
        grid=(B, S // tm, n_cols // tn),
        in_specs=[pl.BlockSpec((1, tm, D), lambda b, i, j: (b, i, 0)),
                  pl.BlockSpec((D, tn), lambda b, i, j: (0, j0 + j))],
        out_specs=pl.BlockSpec((1, tm, tn), lambda b, i, j: (b, i, j)),
        out_shape=jax.ShapeDtypeStruct((B, S, n_cols), F32),
        compiler_params=_cparams(("parallel", "parallel", "arbitrary"), vmem),
        name="proj2",
    )(h, w_in_bf)


def _attn_kernel(q_ref, k_ref, v_ref, o_ref, m_scr, l_scr, acc_scr, *, tk):
    _, G, tq, hd = q_ref.shape
    rows = G * tq
    nk = k_ref.shape[2] // tk
    q = q_ref[0].reshape(rows, hd)
    m_scr[...] = jnp.full_like(m_scr, -jnp.inf)
    l_scr[...] = jnp.zeros_like(l_scr)
    acc_scr[...] = jnp.zeros_like(acc_scr)

    def body(c, carry):
        off = pl.multiple_of(c * tk, tk)
        k = k_ref[0, 0, pl.ds(off, tk), :]
        v = v_ref[0, 0, pl.ds(off, tk), :]
        s = lax.dot_general(q, k, _CONTRACT_LAST, preferred_element_type=F32)
        m_prev = m_scr[...]
        m_new = jnp.maximum(m_prev, jnp.max(s, axis=-1, keepdims=True))
        alpha = jnp.exp2(m_prev - m_new)
        p = jnp.exp2(s - jnp.concatenate([m_new] * (tk // LANES), axis=1))
        l_scr[...] = alpha * l_scr[...] + jnp.sum(p, axis=-1, keepdims=True)
        acc_scr[...] = alpha * acc_scr[...] + jnp.dot(p.astype(v.dtype), v, preferred_element_type=F32)
        m_scr[...] = m_new
        return carry

    lax.fori_loop(0, nk, body, 0)
    out = acc_scr[...] / l_scr[...]
    for g in range(G):
        o_ref[0, :, g * hd:(g + 1) * hd] = out[g * tq:(g + 1) * tq].astype(o_ref.dtype)


def _attention(qkv, attn_w, kv_w, hd):
    B, _, S, _ = qkv.shape
    assert hd == LANES, "running max / sum scratch is kept lane-replicated at head width"
    kvh = kv_w // hd
    G = attn_w // kv_w
    tq = _tile(S, ATTN_Q_TILE)
    tk = _tile(S, ATTN_KV_TILE)
    k0 = attn_w // hd
    v0 = (attn_w + kv_w) // hd
    rows = G * tq
    vmem = 2 * (2 * _nbytes((rows, hd), BF16) + 2 * _nbytes((S, hd), BF16)) + 3 * _nbytes((rows, hd), F32) \
        + 4 * _nbytes((rows, tk), F32)
    return pl.pallas_call(
        functools.partial(_attn_kernel, tk=tk),
        grid=(B, kvh, S // tq),
        in_specs=[pl.BlockSpec((1, G, tq, hd), lambda b, n, i: (b, n, i, 0)),
                  pl.BlockSpec((1, 1, S, hd), lambda b, n, i: (b, k0 + n, 0, 0)),
                  pl.BlockSpec((1, 1, S, hd), lambda b, n, i: (b, v0 + n, 0, 0))],
        out_specs=pl.BlockSpec((1, tq, G * hd), lambda b, n, i: (b, i, n)),
        out_shape=jax.ShapeDtypeStruct((B, S, attn_w), BF16),
        scratch_shapes=[pltpu.VMEM((rows, hd), F32)] * 3,
        compiler_params=_cparams(("parallel", "parallel", "arbitrary"), vmem),
        name="attention",
    )(qkv, qkv, qkv)


def _lru_kernel(*refs, reverse, n_chunks, conv_taps):
    if reverse:
        (cur_ref, prev_ref, next_ref, cw_ref, cb_ref, w_ref, b_ref, lam_ref, hf_ref, yl_ref,
         o_ref, ext_scr, a_scr, u_scr, carry_scr) = refs
    else:
        (cur_ref, prev_ref, next_ref, cw_ref, cb_ref, w_ref, b_ref, lam_ref,
         o_ref, ext_scr, a_scr, u_scr, carry_scr) = refs
    tc, wc = a_scr.shape
    bw = w_ref.shape[1]
    s = pl.program_id(2)
    chunk = (n_chunks - 1 - s) if reverse else s

    ext_scr[0:SUBLANES, :] = jnp.where(chunk > 0, prev_ref[0], 0.0)
    ext_scr[SUBLANES:SUBLANES + tc, :] = cur_ref[0]
    ext_scr[SUBLANES + tc:, :] = jnp.where(chunk < n_chunks - 1, next_ref[0], 0.0)
    xc = cb_ref[...]
    for j in range(conv_taps):
        r0 = SUBLANES - CONV_LEFT + j
        xc = xc + ext_scr[r0:r0 + tc, :] * cw_ref[j:j + 1, :]

    lam = -lam_ref[...]
    softplus_neg_lam = jnp.maximum(lam, 0.0) + jnp.log1p(jnp.exp(-jnp.abs(lam)))
    for n in range(wc // bw):
        cols = slice(n * bw, (n + 1) * bw)
        xb = xc[:, cols]
        z = jnp.dot(xb.astype(BF16), w_ref[n], preferred_element_type=F32)
        r = jax.nn.sigmoid(z[:, :bw] + b_ref[0:1, cols])
        i = jax.nn.sigmoid(z[:, bw:] + b_ref[1:2, cols])
        log_a = -LRU_C * r * softplus_neg_lam[:, cols]
        a = jnp.exp(log_a)
        mult = jnp.sqrt(jnp.maximum(-jnp.tanh(log_a) * (a * a + 1.0), 0.0))
        a_scr[:, cols] = a
        u_scr[:, cols] = mult * (i * xb)

    @pl.when(s == 0)
    def _():
        carry_scr[...] = jnp.zeros_like(carry_scr)

    row = lax.broadcasted_iota(jnp.int32, (SUBLANES, wc), 0)
    n_tiles = tc // SUBLANES

    def tile_body(t, h_in):
        tile = (n_tiles - 1 - t) if reverse else t
        r0 = pl.multiple_of(tile * SUBLANES, SUBLANES)
        a = a_scr[pl.ds(r0, SUBLANES), :]
        u = u_scr[pl.ds(r0, SUBLANES), :]
        for d in (1, 2, 4):
            if reverse:
                valid = row < SUBLANES - d
                shift = SUBLANES - d
            else:
                valid = row >= d
                shift = d
            u = u + a * jnp.where(valid, pltpu.roll(u, shift, 0), 0.0)
            a = a * jnp.where(valid, pltpu.roll(a, shift, 0), 1.0)
        h = u + a * h_in
        u_scr[pl.ds(r0, SUBLANES), :] = h
        edge = h[0:1, :] if reverse else h[SUBLANES - 1:SUBLANES, :]
        return jnp.broadcast_to(edge, (SUBLANES, wc))

    carry_scr[...] = lax.fori_loop(0, n_tiles, tile_body, carry_scr[...], unroll=2)

    if reverse:
        y = yl_ref[0]
        gelu = 0.5 * y * (1.0 + lax.erf(y * (2.0 ** -0.5)))
        o_ref[0] = ((hf_ref[0] + u_scr[...]) * gelu).astype(o_ref.dtype)
    else:
        o_ref[0] = u_scr[...]


def _lru(proj2, conv_w, conv_b, w_gate, b_gate, lam, lru_w, *, reverse, hf=None):
    B, S, _ = proj2.shape
    nb, bw, _ = w_gate.shape
    taps = conv_w.shape[0]
    wc = _tile(lru_w, 512)
    tc = _tile(S, LRU_TIME_CHUNK)
    n_chunks = S // tc
    n_wc = lru_w // wc
    halo_blocks = S // SUBLANES
    per_chunk = tc // SUBLANES

    def t_idx(s):
        return (n_chunks - 1 - s) if reverse else s

    in_specs = [
        pl.BlockSpec((1, tc, wc), lambda b, c, s: (b, t_idx(s), c)),
        pl.BlockSpec((1, SUBLANES, wc), lambda b, c, s: (b, jnp.maximum(t_idx(s) * per_chunk - 1, 0), c)),
        pl.BlockSpec((1, SUBLANES, wc),
                     lambda b, c, s: (b, jnp.minimum((t_idx(s) + 1) * per_chunk, halo_blocks - 1), c)),
        pl.BlockSpec((taps, wc), lambda b, c, s: (0, c)),
        pl.BlockSpec((1, wc), lambda b, c, s: (0, c)),
        pl.BlockSpec((wc // bw, bw, 2 * bw), lambda b, c, s: (c, 0, 0)),
        pl.BlockSpec((2, wc), lambda b, c, s: (0, c)),
        pl.BlockSpec((1, wc), lambda b, c, s: (0, c)),
    ]
    args = [proj2, proj2, proj2, conv_w, conv_b.reshape(1, lru_w), w_gate, b_gate, lam.reshape(1, lru_w)]
    if reverse:
        in_specs += [pl.BlockSpec((1, tc, wc), lambda b, c, s: (b, t_idx(s), c)),
                     pl.BlockSpec((1, tc, wc), lambda b, c, s: (b, t_idx(s), n_wc + c))]
        args += [hf, proj2]
        out_dtype = BF16
    else:
        out_dtype = F32
    blk = _nbytes((tc, wc), F32)
    vmem = 2 * (4 * blk) + 3 * blk + 10 * blk
    return pl.pallas_call(
        functools.partial(_lru_kernel, reverse=reverse, n_chunks=n_chunks, conv_taps=taps),
        grid=(B, n_wc, n_chunks),
        in_specs=in_specs,
        out_specs=pl.BlockSpec((1, tc, wc), lambda b, c, s: (b, t_idx(s), c)),
        out_shape=jax.ShapeDtypeStruct((B, S, lru_w), out_dtype),
        scratch_shapes=[pltpu.VMEM((tc + 2 * SUBLANES, wc), F32),
                        pltpu.VMEM((tc, wc), F32),
                        pltpu.VMEM((tc, wc), F32),
                        pltpu.VMEM((SUBLANES, wc), F32)],
        compiler_params=_cparams(("parallel", "parallel", "arbitrary"), vmem),
        name="lru_bwd" if reverse else "lru_fwd",
    )(*args)


def _merge_kernel(attn_ref, lru_ref, wa_ref, wl_ref, ga_ref, gl_ref, o_ref):
    a = jnp.dot(attn_ref[0], wa_ref[...], preferred_element_type=F32)
    l = jnp.dot(lru_ref[0], wl_ref[...], preferred_element_type=F32)
    o_ref[0] = (ga_ref[0] * a + gl_ref[0] * l).astype(o_ref.dtype)


def _merge(attn, lru, w_attn_o_bf, w_lru_o_bf, proj2, lru_w):
    B, S, attn_w = attn.shape
    D = w_attn_o_bf.shape[1]
    tm = _tile(S, 1024)
    tn = _tile(D, 512, 2 * lru_w)
    ga0 = 2 * lru_w // tn
    gl0 = (2 * lru_w + D) // tn
    vmem = 2 * (_nbytes((tm, attn_w), BF16) + _nbytes((tm, lru_w), BF16) + _nbytes((attn_w, tn), BF16)
                + _nbytes((lru_w, tn), BF16) + 2 * _nbytes((tm, tn), F32) + _nbytes((tm, tn), BF16)) \
        + 3 * _nbytes((tm, tn), F32)
    return pl.pallas_call(
        _merge_kernel,
        grid=(B, S // tm, D // tn),
        in_specs=[pl.BlockSpec((1, tm, attn_w), lambda b, i, j: (b, i, 0)),
                  pl.BlockSpec((1, tm, lru_w), lambda b, i, j: (b, i, 0)),
                  pl.BlockSpec((attn_w, tn), lambda b, i, j: (0, j)),
                  pl.BlockSpec((lru_w, tn), lambda b, i, j: (0, j)),
                  pl.BlockSpec((1, tm, tn), lambda b, i, j: (b, i, ga0 + j)),
                  pl.BlockSpec((1, tm, tn), lambda b, i, j: (b, i, gl0 + j))],
        out_specs=pl.BlockSpec((1, tm, tn), lambda b, i, j: (b, i, j)),
        out_shape=jax.ShapeDtypeStruct((B, S, D), BF16),
        compiler_params=_cparams(("parallel", "parallel", "arbitrary"), vmem),
        name="merge",
    )(attn, lru, w_attn_o_bf, w_lru_o_bf, proj2, proj2)


def _outproj_kernel(m_ref, w_ref, x_ref, g_ref, o_ref):
    o_ref[0] = x_ref[0] + g_ref[0] * jnp.dot(m_ref[0], w_ref[...], preferred_element_type=F32)


def _outproj(merged, w_out_bf, x, mod3, r0, k_gate):
    B, S, D = x.shape
    tm = _tile(S, 1024)
    tn = _tile(D, 1024)
    gpb = D // tn
    vmem = 2 * (_nbytes((tm, D), BF16) + _nbytes((D, tn), BF16) + 2 * _nbytes((tm, tn), F32)) \
        + 2 * _nbytes((tm, tn), F32)
    return pl.pallas_call(
        _outproj_kernel,
        grid=(B, S // tm, D // tn),
        in_specs=[pl.BlockSpec((1, tm, D), lambda b, i, j: (b, i, 0)),
                  pl.BlockSpec((D, tn), lambda b, i, j: (0, j)),
                  pl.BlockSpec((1, tm, tn), lambda b, i, j: (b, i, j)),
                  pl.BlockSpec((1, 1, tn), lambda b, i, j: (r0 + b, 0, k_gate * gpb + j))],
        out_specs=pl.BlockSpec((1, tm, tn), lambda b, i, j: (b, i, j)),
        out_shape=jax.ShapeDtypeStruct((B, S, D), F32),
        compiler_params=_cparams(("parallel", "parallel", "arbitrary"), vmem),
        name="outproj",
    )(merged, w_out_bf, x, mod3)


def _matmul_kernel(a_ref, w_ref, o_ref):
    o_ref[...] = jnp.dot(a_ref[...], w_ref[...], preferred_element_type=F32).astype(o_ref.dtype)


def _matmul(a, w, out_dtype):
    M, K = a.shape
    N = w.shape[1]
    tm = _tile(M, 1024)
    tn = _tile(N, 1024)
    vmem = 2 * (_nbytes((tm, K), a.dtype) + _nbytes((K, tn), w.dtype) + _nbytes((tm, tn), out_dtype)) \
        + 2 * _nbytes((tm, tn), F32)
    return pl.pallas_call(
        _matmul_kernel,
        grid=(M // tm, N // tn),
        in_specs=[pl.BlockSpec((tm, K), lambda i, j: (i, 0)),
                  pl.BlockSpec((K, tn), lambda i, j: (0, j))],
        out_specs=pl.BlockSpec((tm, tn), lambda i, j: (i, j)),
        out_shape=jax.ShapeDtypeStruct((M, N), out_dtype),
        compiler_params=_cparams(("parallel", "arbitrary"), vmem),
        name="peer_query",
    )(a, w)


def _batcher_pairs(n):
    pairs = []

    def merge(lo, hi, r):
        step = r * 2
        if step < hi - lo:
            merge(lo, hi, step)
            merge(lo + r, hi, step)
            pairs.extend((i, i + r) for i in range(lo + r, hi - r, step))
        else:
            pairs.append((lo, lo + r))

    def sort(lo, hi):
        if hi - lo >= 1:
            mid = lo + (hi - lo) // 2
            sort(lo, mid)
            sort(mid + 1, hi)
            merge(lo, hi, 1)

    sort(0, n - 1)
    return pairs


_SORT_PAIRS = _batcher_pairs(PEER_TOPK)


def _sort_desc(v):
    v = list(v)
    for i, j in _SORT_PAIRS:
        v[i], v[j] = jnp.maximum(v[i], v[j]), jnp.minimum(v[i], v[j])
    return v


def _merge_top(a, b):
    k = len(a)
    c = [jnp.maximum(a[i], b[k - 1 - i]) for i in range(k)]
    d = k // 2
    while d >= 1:
        for i in range(k):
            if not i & d:
                c[i], c[i + d] = jnp.maximum(c[i], c[i + d]), jnp.minimum(c[i], c[i + d])
        d //= 2
    return c


def _top_sorted(vals):
    k = PEER_TOPK
    lists = [_sort_desc(vals[g:g + k]) for g in range(0, len(vals), k)]
    while len(lists) > 1:
        nxt = [_merge_top(lists[i], lists[i + 1]) for i in range(0, len(lists) - 1, 2)]
        if len(lists) % 2:
            nxt.append(lists[-1])
        lists = nxt
    return lists[0]


_CAND_PAIRS = [(a, b) for a in range(PEER_TOPK) for b in range(PEER_TOPK) if (a + 1) * (b + 1) <= PEER_TOPK]


def _peer_select_kernel(q_ref, ks_ref, kh_ref, s0_ref, s1_ref, tau_ref, cn_ref, s1_scr, *, n_keys):
    half = ks_ref.shape[2]
    q0 = q_ref[:, :half]
    q1 = q_ref[:, half:]
    s0_ref[...] = lax.dot_general(ks_ref[0], q0, _CONTRACT_LAST, preferred_element_type=F32)
    s1_scr[...] = lax.dot_general(ks_ref[1], q1, _CONTRACT_LAST, preferred_element_type=F32)
    s1_ref[...] = lax.dot_general(kh_ref[...], q1, _CONTRACT_LAST, preferred_element_type=F32)

    def lane_tile(t, carry):
        lanes = pl.ds(pl.multiple_of(t * LANES, LANES), LANES)
        tops = []
        for s_ref in (s0_ref, s1_scr):
            vals = [s_ref[v * SUBLANES:(v + 1) * SUBLANES, lanes] for v in range(n_keys)]
            tops.append(_top_sorted(vals))
        cands = [tops[0][a] + tops[1][b] for a, b in _CAND_PAIRS]
        pad = [jnp.full_like(cands[0], -jnp.inf)] * (-len(cands) % PEER_TOPK)
        tau = _top_sorted(cands + pad)[PEER_TOPK - 1]
        top = tops[0][0] + tops[1][0]
        z = jnp.zeros_like(top)
        for c in cands:
            z = z + jnp.where(c >= tau, jnp.exp(c - top), 0.0)
        tau_ref[:, lanes] = tau
        cn_ref[:, lanes] = top + jnp.log(z)
        return carry

    lax.fori_loop(0, q_ref.shape[0] // LANES, lane_tile, 0)


def _peer_select(q, k_sort, k_head, n_keys):
    T, QW = q.shape
    rows = k_sort.shape[1]
    heads = rows // n_keys
    tm = _tile(T, 512)
    vmem = 2 * (_nbytes((tm, QW), BF16) + _nbytes(k_sort.shape, BF16) + _nbytes(k_head.shape, BF16)
                + 2 * _nbytes((rows, tm), F32) + 2 * _nbytes((heads, tm), F32)) + 4 * _nbytes((rows, tm), F32)
    return pl.pallas_call(
        functools.partial(_peer_select_kernel, n_keys=n_keys),
        grid=(T // tm,),
        in_specs=[pl.BlockSpec((tm, QW), lambda i: (i, 0)),
                  pl.BlockSpec(k_sort.shape, lambda i: (0, 0, 0)),
                  pl.BlockSpec(k_head.shape, lambda i: (0, 0))],
        out_specs=[pl.BlockSpec((rows, tm), lambda i: (0, i)),
                   pl.BlockSpec((rows, tm), lambda i: (0, i)),
                   pl.BlockSpec((heads, tm), lambda i: (0, i)),
                   pl.BlockSpec((heads, tm), lambda i: (0, i))],
        out_shape=[jax.ShapeDtypeStruct((rows, T), F32),
                   jax.ShapeDtypeStruct((rows, T), F32),
                   jax.ShapeDtypeStruct((heads, T), F32),
                   jax.ShapeDtypeStruct((heads, T), F32)],
        scratch_shapes=[pltpu.VMEM((rows, tm), F32)],
        compiler_params=_cparams(("parallel",), vmem),
        name="peer_select",
    )(q, k_sort, k_head)


def _peer_main_kernel(h_ref, u_ref, v_ref, s0_ref, s1_ref, tau_ref, cn_ref, x_ref, g_ref, o_ref, coef_scr, *, n_keys):
    e = pl.program_id(1)
    heads = tau_ref.shape[0]
    eb = u_ref.shape[0]

    @pl.when(e == 0)
    def _():
        o_ref[...] = jnp.zeros_like(o_ref)

    act_t = lax.dot_general(u_ref[...], h_ref[...], _CONTRACT_LAST, preferred_element_type=F32)
    for ii in range(eb // n_keys):
        w = None
        for h in range(heads):
            sm = s0_ref[ii * heads + h:ii * heads + h + 1, :] + s1_ref[h * n_keys:(h + 1) * n_keys, :]
            term = jnp.where(sm >= tau_ref[h:h + 1, :], jnp.exp(sm - cn_ref[h:h + 1, :]), 0.0)
            w = term if w is None else w + term
        a = act_t[ii * n_keys:(ii + 1) * n_keys, :]
        gelu = 0.5 * a * (1.0 + lax.erf(a * (2.0 ** -0.5)))
        coef_scr[ii * n_keys:(ii + 1) * n_keys, :] = (w * gelu).astype(coef_scr.dtype)
    o_ref[...] += lax.dot_general(coef_scr[...], v_ref[...], _CONTRACT_FIRST, preferred_element_type=F32)

    @pl.when(e == pl.num_programs(1) - 1)
    def _():
        o_ref[...] = x_ref[...] + g_ref[0] * o_ref[...]


def _peer_main(h2, u_bf, v_bf, s0, s1, tau, cn, x1, mod3, r0, k_gate, n_keys):
    B, S, D = x1.shape
    T = B * S
    NE = u_bf.shape[0]
    heads = tau.shape[0]
    rows = s1.shape[0]
    tm = _tile(S, 512)
    eb = _tile(NE, 512)
    assert eb % n_keys == 0
    s0_rows = eb // n_keys * heads
    once = pl.Buffered(1)
    vmem = _nbytes((tm, D), BF16) + _nbytes((rows, tm), F32) + _nbytes((tm, D), F32) \
        + 2 * (2 * _nbytes((eb, D), BF16) + _nbytes((tm, D), F32)) + 8 * _nbytes((eb, tm), F32)
    return pl.pallas_call(
        functools.partial(_peer_main_kernel, n_keys=n_keys),
        grid=(T // tm, NE // eb),
        in_specs=[pl.BlockSpec((tm, D), lambda i, e: (i, 0), pipeline_mode=once),
                  pl.BlockSpec((eb, D), lambda i, e: (e, 0)),
                  pl.BlockSpec((eb, D), lambda i, e: (e, 0)),
                  pl.BlockSpec((s0_rows, tm), lambda i, e: (e, i)),
                  pl.BlockSpec((rows, tm), lambda i, e: (0, i), pipeline_mode=once),
                  pl.BlockSpec((heads, tm), lambda i, e: (0, i)),
                  pl.BlockSpec((heads, tm), lambda i, e: (0, i)),
                  pl.BlockSpec((tm, D), lambda i, e: (i, 0), pipeline_mode=once),
                  pl.BlockSpec((1, 1, D), lambda i, e: (r0 + i * tm // S, 0, k_gate))],
        out_specs=pl.BlockSpec((tm, D), lambda i, e: (i, 0)),
        out_shape=jax.ShapeDtypeStruct((T, D), F32),
        scratch_shapes=[pltpu.VMEM((eb, tm), BF16)],
        compiler_params=_cparams(("parallel", "arbitrary"), vmem),
        name="peer_main",
    )(h2, u_bf, v_bf, s0, s1, tau, cn, x1.reshape(T, D), mod3).reshape(B, S, D)


def _rope_tables(seq_len, hd):
    rows = jnp.repeat(jnp.arange(seq_len // GRID_W), GRID_W).astype(F32)
    cols = jnp.tile(jnp.arange(GRID_W), seq_len // GRID_W).astype(F32)
    axis_dim = hd // 2
    inv_freq = ROPE_THETA ** (-jnp.arange(0, axis_dim, 2, dtype=F32) / axis_dim)
    ang_r = rows[:, None] * inv_freq
    ang_c = cols[:, None] * inv_freq
    cos_t = jnp.concatenate([jnp.cos(ang_r), jnp.cos(ang_r), jnp.cos(ang_c), jnp.cos(ang_c)], axis=-1)
    sin_t = jnp.concatenate([-jnp.sin(ang_r), jnp.sin(ang_r), -jnp.sin(ang_c), jnp.sin(ang_c)], axis=-1)
    return cos_t, sin_t


def _block_keys(keys):
    H, _, NK, C = keys.shape
    eye = jnp.eye(H, dtype=keys.dtype)
    k_sort = jnp.einsum('hpkc,hg->pkhgc', keys, eye).reshape(2, NK * H, H * C)
    k_head = jnp.einsum('hkc,hg->hkgc', keys[:, 1], eye).reshape(H * NK, H * C)
    return k_sort.astype(BF16), k_head.astype(BF16)


def _encoder_layer(x, r0, mod3, p):
    B, S, D = x.shape
    hd, attn_w, kv_w, lru_w = p['hd'], p['attn_w'], p['kv_w'], p['lru_w']
    cos_t, sin_t = _rope_tables(S, hd)

    h = _modnorm(x, mod3, r0, p['g_norm1'], 0, 1)
    qkv = _qkv_proj(h, p['w_in'], p['gq'], p['gk'], cos_t, sin_t, attn_w, kv_w, hd)
    proj2 = _proj2(h, p['w_in'], attn_w + 2 * kv_w, lru_w)
    attn = _attention(qkv, attn_w, kv_w, hd)
    hf = _lru(proj2, p['conv_w'], p['conv_b'], p['w_gate'][0], p['b_gate'][0], p['lam'][0], lru_w,
              reverse=False)
    lru = _lru(proj2, p['conv_w'], p['conv_b'], p['w_gate'][1], p['b_gate'][1], p['lam'][1], lru_w,
               reverse=True, hf=hf)
    merged = _merge(attn, lru, p['w_attn_o'], p['w_lru_o'], proj2, lru_w)
    x1 = _outproj(merged, p['w_out'], x, mod3, r0, 2)

    h2 = _modnorm(x1, mod3, r0, p['g_norm2'], 3, 4).reshape(B * S, D)
    q = _matmul(h2, p['peer_wq'], BF16)
    s0, s1, tau, cn = _peer_select(q, p['k_sort'], p['k_head'], p['n_keys'])
    return _peer_main(h2, p['peer_u'], p['peer_v'], s0, s1, tau, cn, x1, mod3, r0, 5, p['n_keys'])


def kernel(x_prompt, x_sample, c_prompt, c_sample, w_ada, b_ada, g_norm1, g_norm2, w_in, q_gain, k_gain,
           conv_w, conv_b, lru_lam, lru_wa, lru_ba, lru_wi, lru_bi, w_attn_o, w_lru_o, w_out,
           peer_wq, peer_keys, peer_u, peer_v):
    depth = w_ada.shape[0]
    D = x_prompt.shape[-1]
    hd = q_gain.shape[-1]
    attn_w = w_attn_o.shape[1]
    lru_w = w_lru_o.shape[1]
    kv_w = (w_in.shape[2] - attn_w - 2 * lru_w - 2 * D) // 2
    heads, _, n_keys, half = peer_keys.shape[1:]
    assert heads == SUBLANES, "selection layout puts one expert head per sublane"

    n_p, n_s = c_prompt.shape[0], c_sample.shape[0]
    rows = -(-(n_p + n_s) // SUBLANES) * SUBLANES
    c_rows = jnp.zeros((rows, D), F32).at[:n_p].set(c_prompt).at[n_p:n_p + n_s].set(c_sample)

    y_p, y_s = x_prompt, x_sample
    for l in range(depth):
        wq = peer_wq[l].reshape(D, heads, 2, half).transpose(0, 2, 1, 3).reshape(D, 2 * heads * half)
        k_sort, k_head = _block_keys(peer_keys[l])
        p = dict(
            hd=hd, attn_w=attn_w, kv_w=kv_w, lru_w=lru_w, n_keys=n_keys,
            g_norm1=g_norm1[l], g_norm2=g_norm2[l],
            w_in=w_in[l].astype(BF16),
            gq=(q_gain[l] * (hd ** -0.5 * math.log2(math.e))).reshape(1, hd), gk=k_gain[l].reshape(1, hd),
            conv_w=conv_w[l], conv_b=conv_b[l], lam=lru_lam[l],
            w_gate=jnp.concatenate([lru_wa[l], lru_wi[l]], axis=-1).astype(BF16),
            b_gate=jnp.stack([lru_ba[l], lru_bi[l]], axis=1),
            w_attn_o=w_attn_o[l].astype(BF16), w_lru_o=w_lru_o[l].astype(BF16), w_out=w_out[l].astype(BF16),
            peer_wq=wq.astype(BF16), k_sort=k_sort, k_head=k_head,
            peer_u=peer_u[l].astype(BF16), peer_v=peer_v[l].astype(BF16),
        )
        mod = _adaln(c_rows, w_ada[l], b_ada[l])
        mod3 = mod.reshape(rows, 1, N_MOD * D)
        y_p = _encoder_layer(y_p, 0, mod3, p)
        y_s = _encoder_layer(y_s, n_p, mod3, p)
    return (y_p, y_s)
```

```python
import functools
import math

import jax
import jax.numpy as jnp
from jax import lax
from jax.experimental import pallas as pl
from jax.experimental.pallas import tpu as pltpu

F32 = jnp.float32
BF16 = jnp.bfloat16

EPS = 1e-6
ROPE_THETA = 10000.0
GRID_W = 64
LRU_C = 8.0
CONV_LEFT = 2
PEER_TOPK = 16
N_MOD = 6

V7X_VMEM_BYTES = 64 * 1024 * 1024
V7X_VMEM_RESERVE = 8 * 1024 * 1024
LANES = 128
SUBLANES = 8

LRU_TIME_CHUNK = 1024
ATTN_Q_TILE = 512
ATTN_KV_TILE = 512
PEER_TOKEN_TILE = 1024

_CONTRACT_LAST = (((1,), (1,)), ((), ()))
_CONTRACT_FIRST = (((0,), (0,)), ((), ()))


def _cparams(semantics, vmem_bytes):
    limit = min(int(vmem_bytes), V7X_VMEM_BYTES - V7X_VMEM_RESERVE)
    return pltpu.CompilerParams(dimension_semantics=semantics, vmem_limit_bytes=limit)


def _tile(n, pref, *also):
    g = math.gcd(n, *also)
    for t in range(min(pref, g) // LANES * LANES, 0, -LANES):
        if g % t == 0:
            return t
    return g


def _nbytes(shape, dtype):
    return math.prod(shape) * jnp.dtype(dtype).itemsize


def _ada_kernel(c_ref, w_ref, b_ref, o_ref):
    c = c_ref[...]
    s = c * jax.nn.sigmoid(c)
    o_ref[...] = jnp.dot(s, w_ref[...], preferred_element_type=F32,
                         precision=lax.Precision.HIGHEST) + b_ref[...]


def _adaln(c_rows, w_ada, b_ada):
    R, D = c_rows.shape
    N = w_ada.shape[1]
    tn = _tile(N, 512)
    vmem = 2 * (_nbytes((R, D), F32) + _nbytes((D, tn), F32) + 2 * _nbytes((R, tn), F32)) \
        + 3 * _nbytes((D, tn), F32)
    return pl.pallas_call(
        _ada_kernel,
        grid=(N // tn,),
        in_specs=[pl.BlockSpec((R, D), lambda j: (0, 0)),
                  pl.BlockSpec((D, tn), lambda j: (0, j)),
                  pl.BlockSpec((1, tn), lambda j: (0, j))],
        out_specs=pl.BlockSpec((R, tn), lambda j: (0, j)),
        out_shape=jax.ShapeDtypeStruct((R, N), F32),
        compiler_params=_cparams(("parallel",), vmem),
        name="adaln",
    )(c_rows, w_ada, b_ada.reshape(1, N))


def _modnorm_kernel(x_ref, g_ref, sh_ref, sc_ref, o_ref):
    x = x_ref[0]
    y = x * lax.rsqrt(jnp.mean(x * x, axis=-1, keepdims=True) + EPS) * g_ref[...]
    o_ref[0] = (y * (1.0 + sc_ref[0]) + sh_ref[0]).astype(o_ref.dtype)


def _modnorm(x, mod3, r0, g, k_shift, k_scale):
    B, S, D = x.shape
    ts = _tile(S, 512)
    vmem = 2 * (_nbytes((ts, D), F32) + _nbytes((ts, D), BF16)) + 3 * _nbytes((ts, D), F32)
    return pl.pallas_call(
        _modnorm_kernel,
        grid=(B, S // ts),
        in_specs=[pl.BlockSpec((1, ts, D), lambda b, s: (b, s, 0)),
                  pl.BlockSpec((1, D), lambda b, s: (0, 0)),
                  pl.BlockSpec((1, 1, D), lambda b, s: (r0 + b, 0, k_shift)),
                  pl.BlockSpec((1, 1, D), lambda b, s: (r0 + b, 0, k_scale))],
        out_specs=pl.BlockSpec((1, ts, D), lambda b, s: (b, s, 0)),
        out_shape=jax.ShapeDtypeStruct((B, S, D), BF16),
        compiler_params=_cparams(("parallel", "parallel"), vmem),
        name="modnorm",
    )(x, g.reshape(1, D), mod3, mod3)


def _qkv_kernel(h_ref, w_ref, gq_ref, gk_ref, cos_ref, sin_ref, o_ref, *, n_q, n_qk, hd):
    j = pl.program_id(2)
    acc = jnp.dot(h_ref[0], w_ref[...], preferred_element_type=F32)
    tn = acc.shape[1]

    @pl.when(j < n_qk)
    def _():
        gain = jnp.where(j < n_q, gq_ref[...], gk_ref[...])
        cos = cos_ref[...]
        sin = sin_ref[...]
        lane = lax.broadcasted_iota(jnp.int32, (1, hd), 1)
        first = (lane % (hd // 2)) < (hd // 4)
        for hh in range(tn // hd):
            y = acc[:, hh * hd:(hh + 1) * hd]
            y = y * lax.rsqrt(jnp.mean(y * y, axis=-1, keepdims=True) + EPS) * gain
            sw = jnp.where(first, pltpu.roll(y, hd - hd // 4, 1), pltpu.roll(y, hd // 4, 1))
            o_ref[0, hh] = (y * cos + sw * sin).astype(o_ref.dtype)

    @pl.when(j >= n_qk)
    def _():
        for hh in range(tn // hd):
            o_ref[0, hh] = acc[:, hh * hd:(hh + 1) * hd].astype(o_ref.dtype)


def _qkv_proj(h, w_in_bf, gq, gk, cos_t, sin_t, attn_w, kv_w, hd):
    B, S, D = h.shape
    n_cols = attn_w + 2 * kv_w
    tn = _tile(attn_w, 512, kv_w)
    tm = _tile(S, 1024)
    vmem = 2 * (_nbytes((tm, D), BF16) + _nbytes((D, tn), BF16) + _nbytes((tm, tn), BF16)
                + 2 * _nbytes((tm, hd), F32)) + 4 * _nbytes((tm, tn), F32)
    kern = functools.partial(_qkv_kernel, n_q=attn_w // tn, n_qk=(attn_w + kv_w) // tn, hd=hd)
    return pl.pallas_call(
        kern,
        grid=(B, S // tm, n_cols // tn),
        in_specs=[pl.BlockSpec((1, tm, D), lambda b, i, j: (b, i, 0)),
                  pl.BlockSpec((D, tn), lambda b, i, j: (0, j)),
                  pl.BlockSpec((1, hd), lambda b, i, j: (0, 0)),
                  pl.BlockSpec((1, hd), lambda b, i, j: (0, 0)),
                  pl.BlockSpec((tm, hd), lambda b, i, j: (i, 0)),
                  pl.BlockSpec((tm, hd), lambda b, i, j: (i, 0))],
        out_specs=pl.BlockSpec((1, tn // hd, tm, hd), lambda b, i, j: (b, j, i, 0)),
        out_shape=jax.ShapeDtypeStruct((B, n_cols // hd, S, hd), BF16),
        compiler_params=_cparams(("parallel", "parallel", "arbitrary"), vmem),
        name="qkv_proj",
    )(h, w_in_bf, gq, gk, cos_t, sin_t)


def _proj2_kernel(h_ref, w_ref, o_ref, *, n_plain):
    j = pl.program_id(2)
    acc = jnp.dot(h_ref[0], w_ref[...], preferred_element_type=F32)

    @pl.when(j < n_plain)
    def _():
        o_ref[0] = acc

    @pl.when(j >= n_plain)
    def _():
        o_ref[0] = jax.nn.sigmoid(acc)


def _proj2(h, w_in_bf, col0, lru_w):
    B, S, D = h.shape
    n_cols = 2 * lru_w + 2 * D
    tn = _tile(2 * lru_w, 1024, 2 * D, col0)
    tm = _tile(S, 1024)
    j0 = col0 // tn
    vmem = 2 * (_nbytes((tm, D), BF16) + _nbytes((D, tn), BF16) + _nbytes((tm, tn), F32)) \
        + 4 * _nbytes((tm, tn), F32)
    return pl.pallas_call(
        functools.partial(_proj2_kernel, n_plain=2 * lru_w // tn),
        grid=(B, S // tm, n_cols // tn),
        in_specs=[pl.BlockSpec((1, tm, D), lambda b, i, j: (b, i, 0)),
                  pl.BlockSpec((D, tn), lambda b, i, j: (0, j0 + j))],
        out_specs=pl.BlockSpec((1, tm, tn), lambda b, i, j: (b, i, j)),
        out_shape=jax.ShapeDtypeStruct((B, S, n_cols), F32),
        compiler_params=_cparams(("parallel", "parallel", "arbitrary"), vmem),
        name="proj2",
    )(h, w_in_bf)


def _attn_kernel(q_ref, k_ref, v_ref, o_ref, m_scr, l_scr, acc_scr, *, tk):
    _, G, tq, hd = q_ref.shape
    rows = G * tq
    nk = k_ref.shape[2] // tk
    q = q_ref[0].reshape(rows, hd)
    m_scr[...] = jnp.full_like(m_scr, -jnp.inf)
    l_scr[...] = jnp.zeros_like(l_scr)
    acc_scr[...] = jnp.zeros_like(acc_scr)

    def body(c, carry):
        off = pl.multiple_of(c * tk, tk)
        k = k_ref[0, 0, pl.ds(off, tk), :]
        v = v_ref[0, 0, pl.ds(off, tk), :]
        s = lax.dot_general(q, k, _CONTRACT_LAST, preferred_element_type=F32)
        m_prev = m_scr[...]
        m_new = jnp.maximum(m_prev, jnp.max(s, axis=-1, keepdims=True))
        alpha = jnp.exp2(m_prev - m_new)
        p = jnp.exp2(s - jnp.concatenate([m_new] * (tk // LANES), axis=1))
        l_scr[...] = alpha * l_scr[...] + jnp.sum(p, axis=-1, keepdims=True)
        acc_scr[...] = alpha * acc_scr[...] + jnp.dot(p.astype(v.dtype), v, preferred_element_type=F32)
        m_scr[...] = m_new
        return carry

    lax.fori_loop(0, nk, body, 0)
    out = acc_scr[...] / l_scr[...]
    for g in range(G):
        o_ref[0, :, g * hd:(g + 1) * hd] = out[g * tq:(g + 1) * tq].astype(o_ref.dtype)


def _attention(qkv, attn_w, kv_w, hd):
    B, _, S, _ = qkv.shape
    assert hd == LANES, "running max / sum scratch is kept lane-replicated at head width"
    kvh = kv_w // hd
    G = attn_w // kv_w
    tq = _tile(S, ATTN_Q_TILE)
    tk = _tile(S, ATTN_KV_TILE)
    k0 = attn_w // hd
    v0 = (attn_w + kv_w) // hd
    rows = G * tq
    vmem = 2 * (2 * _nbytes((rows, hd), BF16) + 2 * _nbytes((S, hd), BF16)) + 3 * _nbytes((rows, hd), F32) \
        + 4 * _nbytes((rows, tk), F32)
    return pl.pallas_call(
        functools.partial(_attn_kernel, tk=tk),
        grid=(B, kvh, S // tq),
        in_specs=[pl.BlockSpec((1, G, tq, hd), lambda b, n, i: (b, n, i, 0)),
                  pl.BlockSpec((1, 1, S, hd), lambda b, n, i: (b, k0 + n, 0, 0)),
                  pl.BlockSpec((1, 1, S, hd), lambda b, n, i: (b, v0 + n, 0, 0))],
        out_specs=pl.BlockSpec((1, tq, G * hd), lambda b, n, i: (b, i, n)),
        out_shape=jax.ShapeDtypeStruct((B, S, attn_w), BF16),
        scratch_shapes=[pltpu.VMEM((rows, hd), F32)] * 3,
        compiler_params=_cparams(("parallel", "parallel", "arbitrary"), vmem),
        name="attention",
    )(qkv, qkv, qkv)


def _lru_kernel(*refs, reverse, n_chunks, conv_taps):
    if reverse:
        (cur_ref, prev_ref, next_ref, cw_ref, cb_ref, w_ref, b_ref, lam_ref, hf_ref, yl_ref,
         o_ref, ext_scr, a_scr, u_scr, carry_scr) = refs
    else:
        (cur_ref, prev_ref, next_ref, cw_ref, cb_ref, w_ref, b_ref, lam_ref,
         o_ref, ext_scr, a_scr, u_scr, carry_scr) = refs
    tc, wc = a_scr.shape
    bw = w_ref.shape[1]
    s = pl.program_id(2)
    chunk = (n_chunks - 1 - s) if reverse else s

    ext_scr[0:SUBLANES, :] = jnp.where(chunk > 0, prev_ref[0], 0.0)
    ext_scr[SUBLANES:SUBLANES + tc, :] = cur_ref[0]
    ext_scr[SUBLANES + tc:, :] = jnp.where(chunk < n_chunks - 1, next_ref[0], 0.0)
    xc = cb_ref[...]
    for j in range(conv_taps):
        r0 = SUBLANES - CONV_LEFT + j
        xc = xc + ext_scr[r0:r0 + tc, :] * cw_ref[j:j + 1, :]

    lam = -lam_ref[...]
    softplus_neg_lam = jnp.maximum(lam, 0.0) + jnp.log1p(jnp.exp(-jnp.abs(lam)))
    for n in range(wc // bw):
        cols = slice(n * bw, (n + 1) * bw)
        xb = xc[:, cols]
        z = jnp.dot(xb.astype(BF16), w_ref[n], preferred_element_type=F32)
        r = jax.nn.sigmoid(z[:, :bw] + b_ref[0:1, cols])
        i = jax.nn.sigmoid(z[:, bw:] + b_ref[1:2, cols])
        log_a = -LRU_C * r * softplus_neg_lam[:, cols]
        a = jnp.exp(log_a)
        mult = jnp.sqrt(jnp.maximum(-jnp.tanh(log_a) * (a * a + 1.0), 0.0))
        a_scr[:, cols] = a
        u_scr[:, cols] = mult * (i * xb)

    @pl.when(s == 0)
    def _():
        carry_scr[...] = jnp.zeros_like(carry_scr)

    row = lax.broadcasted_iota(jnp.int32, (SUBLANES, wc), 0)
    n_tiles = tc // SUBLANES

    def tile_body(t, h_in):
        tile = (n_tiles - 1 - t) if reverse else t
        r0 = pl.multiple_of(tile * SUBLANES, SUBLANES)
        a = a_scr[pl.ds(r0, SUBLANES), :]
        u = u_scr[pl.ds(r0, SUBLANES), :]
        for d in (1, 2, 4):
            if reverse:
                valid = row < SUBLANES - d
                shift = SUBLANES - d
            else:
                valid = row >= d
                shift = d
            u = u + a * jnp.where(valid, pltpu.roll(u, shift, 0), 0.0)
            a = a * jnp.where(valid, pltpu.roll(a, shift, 0), 1.0)
        h = u + a * h_in
        u_scr[pl.ds(r0, SUBLANES), :] = h
        edge = h[0:1, :] if reverse else h[SUBLANES - 1:SUBLANES, :]
        return jnp.broadcast_to(edge, (SUBLANES, wc))

    carry_scr[...] = lax.fori_loop(0, n_tiles, tile_body, carry_scr[...], unroll=2)

    if reverse:
        y = yl_ref[0]
        gelu = 0.5 * y * (1.0 + lax.erf(y * (2.0 ** -0.5)))
        o_ref[0] = ((hf_ref[0] + u_scr[...]) * gelu).astype(o_ref.dtype)
    else:
        o_ref[0] = u_scr[...]


def _lru(proj2, conv_w, conv_b, w_gate, b_gate, lam, lru_w, *, reverse, hf=None):
    B, S, _ = proj2.shape
    nb, bw, _ = w_gate.shape
    taps = conv_w.shape[0]
    wc = _tile(lru_w, 512)
    tc = _tile(S, LRU_TIME_CHUNK)
    n_chunks = S // tc
    n_wc = lru_w // wc
    halo_blocks = S // SUBLANES
    per_chunk = tc // SUBLANES

    def t_idx(s):
        return (n_chunks - 1 - s) if reverse else s

    in_specs = [
        pl.BlockSpec((1, tc, wc), lambda b, c, s: (b, t_idx(s), c)),
        pl.BlockSpec((1, SUBLANES, wc), lambda b, c, s: (b, jnp.maximum(t_idx(s) * per_chunk - 1, 0), c)),
        pl.BlockSpec((1, SUBLANES, wc),
                     lambda b, c, s: (b, jnp.minimum((t_idx(s) + 1) * per_chunk, halo_blocks - 1), c)),
        pl.BlockSpec((taps, wc), lambda b, c, s: (0, c)),
        pl.BlockSpec((1, wc), lambda b, c, s: (0, c)),
        pl.BlockSpec((wc // bw, bw, 2 * bw), lambda b, c, s: (c, 0, 0)),
        pl.BlockSpec((2, wc), lambda b, c, s: (0, c)),
        pl.BlockSpec((1, wc), lambda b, c, s: (0, c)),
    ]
    args = [proj2, proj2, proj2, conv_w, conv_b.reshape(1, lru_w), w_gate, b_gate, lam.reshape(1, lru_w)]
    if reverse:
        in_specs += [pl.BlockSpec((1, tc, wc), lambda b, c, s: (b, t_idx(s), c)),
                     pl.BlockSpec((1, tc, wc), lambda b, c, s: (b, t_idx(s), n_wc + c))]
        args += [hf, proj2]
        out_dtype = BF16
    else:
        out_dtype = F32
    blk = _nbytes((tc, wc), F32)
    vmem = 2 * (4 * blk) + 3 * blk + 10 * blk
    return pl.pallas_call(
        functools.partial(_lru_kernel, reverse=reverse, n_chunks=n_chunks, conv_taps=taps),
        grid=(B, n_wc, n_chunks),
        in_specs=in_specs,
        out_specs=pl.BlockSpec((1, tc, wc), lambda b, c, s: (b, t_idx(s), c)),
        out_shape=jax.ShapeDtypeStruct((B, S, lru_w), out_dtype),
        scratch_shapes=[pltpu.VMEM((tc + 2 * SUBLANES, wc), F32),
                        pltpu.VMEM((tc, wc), F32),
                        pltpu.VMEM((tc, wc), F32),
                        pltpu.VMEM((SUBLANES, wc), F32)],
        compiler_params=_cparams(("parallel", "parallel", "arbitrary"), vmem),
        name="lru_bwd" if reverse else "lru_fwd",
    )(*args)


def _merge_kernel(attn_ref, lru_ref, wa_ref, wl_ref, ga_ref, gl_ref, o_ref):
    a = jnp.dot(attn_ref[0], wa_ref[...], preferred_element_type=F32)
    l = jnp.dot(lru_ref[0], wl_ref[...], preferred_element_type=F32)
    o_ref[0] = (ga_ref[0] * a + gl_ref[0] * l).astype(o_ref.dtype)


def _merge(attn, lru, w_attn_o_bf, w_lru_o_bf, proj2, lru_w):
    B, S, attn_w = attn.shape
    D = w_attn_o_bf.shape[1]
    tm = _tile(S, 1024)
    tn = _tile(D, 512, 2 * lru_w)
    ga0 = 2 * lru_w // tn
    gl0 = (2 * lru_w + D) // tn
    vmem = 2 * (_nbytes((tm, attn_w), BF16) + _nbytes((tm, lru_w), BF16) + _nbytes((attn_w, tn), BF16)
                + _nbytes((lru_w, tn), BF16) + 2 * _nbytes((tm, tn), F32) + _nbytes((tm, tn), BF16)) \
        + 3 * _nbytes((tm, tn), F32)
    return pl.pallas_call(
        _merge_kernel,
        grid=(B, S // tm, D // tn),
        in_specs=[pl.BlockSpec((1, tm, attn_w), lambda b, i, j: (b, i, 0)),
                  pl.BlockSpec((1, tm, lru_w), lambda b, i, j: (b, i, 0)),
                  pl.BlockSpec((attn_w, tn), lambda b, i, j: (0, j)),
                  pl.BlockSpec((lru_w, tn), lambda b, i, j: (0, j)),
                  pl.BlockSpec((1, tm, tn), lambda b, i, j: (b, i, ga0 + j)),
                  pl.BlockSpec((1, tm, tn), lambda b, i, j: (b, i, gl0 + j))],
        out_specs=pl.BlockSpec((1, tm, tn), lambda b, i, j: (b, i, j)),
        out_shape=jax.ShapeDtypeStruct((B, S, D), BF16),
        compiler_params=_cparams(("parallel", "parallel", "arbitrary"), vmem),
        name="merge",
    )(attn, lru, w_attn_o_bf, w_lru_o_bf, proj2, proj2)


def _outproj_kernel(m_ref, w_ref, x_ref, g_ref, o_ref):
    o_ref[0] = x_ref[0] + g_ref[0] * jnp.dot(m_ref[0], w_ref[...], preferred_element_type=F32)


def _outproj(merged, w_out_bf, x, mod3, r0, k_gate):
    B, S, D = x.shape
    tm = _tile(S, 1024)
    tn = _tile(D, 1024)
    gpb = D // tn
    vmem = 2 * (_nbytes((tm, D), BF16) + _nbytes((D, tn), BF16) + 2 * _nbytes((tm, tn), F32)) \
        + 2 * _nbytes((tm, tn), F32)
    return pl.pallas_call(
        _outproj_kernel,
        grid=(B, S // tm, D // tn),
        in_specs=[pl.BlockSpec((1, tm, D), lambda b, i, j: (b, i, 0)),
                  pl.BlockSpec((D, tn), lambda b, i, j: (0, j)),
                  pl.BlockSpec((1, tm, tn), lambda b, i, j: (b, i, j)),
                  pl.BlockSpec((1, 1, tn), lambda b, i, j: (r0 + b, 0, k_gate * gpb + j))],
        out_specs=pl.BlockSpec((1, tm, tn), lambda b, i, j: (b, i, j)),
        out_shape=jax.ShapeDtypeStruct((B, S, D), F32),
        compiler_params=_cparams(("parallel", "parallel", "arbitrary"), vmem),
        name="outproj",
    )(merged, w_out_bf, x, mod3)


def _matmul_kernel(a_ref, w_ref, o_ref):
    o_ref[...] = jnp.dot(a_ref[...], w_ref[...], preferred_element_type=F32).astype(o_ref.dtype)


def _matmul(a, w, out_dtype):
    M, K = a.shape
    N = w.shape[1]
    tm = _tile(M, 1024)
    tn = _tile(N, 1024)
    vmem = 2 * (_nbytes((tm, K), a.dtype) + _nbytes((K, tn), w.dtype) + _nbytes((tm, tn), out_dtype)) \
        + 2 * _nbytes((tm, tn), F32)
    return pl.pallas_call(
        _matmul_kernel,
        grid=(M // tm, N // tn),
        in_specs=[pl.BlockSpec((tm, K), lambda i, j: (i, 0)),
                  pl.BlockSpec((K, tn), lambda i, j: (0, j))],
        out_specs=pl.BlockSpec((tm, tn), lambda i, j: (i, j)),
        out_shape=jax.ShapeDtypeStruct((M, N), out_dtype),
        compiler_params=_cparams(("parallel", "arbitrary"), vmem),
        name="peer_query",
    )(a, w)


def _batcher_pairs(n):
    pairs = []

    def merge(lo, hi, r):
        step = r * 2
        if step < hi - lo:
            merge(lo, hi, step)
            merge(lo + r, hi, step)
            pairs.extend((i, i + r) for i in range(lo + r, hi - r, step))
        else:
            pairs.append((lo, lo + r))

    def sort(lo, hi):
        if hi - lo >= 1:
            mid = lo + (hi - lo) // 2
            sort(lo, mid)
            sort(mid + 1, hi)
            merge(lo, hi, 1)

    sort(0, n - 1)
    return pairs


_SORT_PAIRS = _batcher_pairs(PEER_TOPK)


def _sort_desc(v):
    v = list(v)
    for i, j in _SORT_PAIRS:
        v[i], v[j] = jnp.maximum(v[i], v[j]), jnp.minimum(v[i], v[j])
    return v


def _merge_top(a, b):
    k = len(a)
    c = [jnp.maximum(a[i], b[k - 1 - i]) for i in range(k)]
    d = k // 2
    while d >= 1:
        for i in range(k):
            if not i & d:
                c[i], c[i + d] = jnp.maximum(c[i], c[i + d]), jnp.minimum(c[i], c[i + d])
        d //= 2
    return c


def _top_sorted(vals):
    k = PEER_TOPK
    lists = [_sort_desc(vals[g:g + k]) for g in range(0, len(vals), k)]
    while len(lists) > 1:
        nxt = [_merge_top(lists[i], lists[i + 1]) for i in range(0, len(lists) - 1, 2)]
        if len(lists) % 2:
            nxt.append(lists[-1])
        lists = nxt
    return lists[0]


_CAND_PAIRS = [(a, b) for a in range(PEER_TOPK) for b in range(PEER_TOPK) if (a + 1) * (b + 1) <= PEER_TOPK]


def _peer_select_kernel(q_ref, ks_ref, kh_ref, s0_ref, s1_ref, tau_ref, cn_ref, s1_scr, *, n_keys):
    half = ks_ref.shape[2]
    q0 = q_ref[:, :half]
    q1 = q_ref[:, half:]
    s0_ref[...] = lax.dot_general(ks_ref[0], q0, _CONTRACT_LAST, preferred_element_type=F32)
    s1_scr[...] = lax.dot_general(ks_ref[1], q1, _CONTRACT_LAST, preferred_element_type=F32)
    s1_ref[...] = lax.dot_general(kh_ref[...], q1, _CONTRACT_LAST, preferred_element_type=F32)

    def lane_tile(t, carry):
        lanes = pl.ds(pl.multiple_of(t * LANES, LANES), LANES)
        tops = []
        for s_ref in (s0_ref, s1_scr):
            vals = [s_ref[v * SUBLANES:(v + 1) * SUBLANES, lanes] for v in range(n_keys)]
            tops.append(_top_sorted(vals))
        cands = [tops[0][a] + tops[1][b] for a, b in _CAND_PAIRS]
        pad = [jnp.full_like(cands[0], -jnp.inf)] * (-len(cands) % PEER_TOPK)
        tau = _top_sorted(cands + pad)[PEER_TOPK - 1]
        top = tops[0][0] + tops[1][0]
        z = jnp.zeros_like(top)
        for c in cands:
            z = z + jnp.where(c >= tau, jnp.exp(c - top), 0.0)
        tau_ref[:, lanes] = tau
        cn_ref[:, lanes] = top + jnp.log(z)
        return carry

    lax.fori_loop(0, q_ref.shape[0] // LANES, lane_tile, 0)


def _peer_select(q, k_sort, k_head, n_keys):
    T, QW = q.shape
    rows = k_sort.shape[1]
    heads = rows // n_keys
    tm = _tile(T, 512)
    vmem = 2 * (_nbytes((tm, QW), BF16) + _nbytes(k_sort.shape, BF16) + _nbytes(k_head.shape, BF16)
                + 2 * _nbytes((rows, tm), F32) + 2 * _nbytes((heads, tm), F32)) + 4 * _nbytes((rows, tm), F32)
    return pl.pallas_call(
        functools.partial(_peer_select_kernel, n_keys=n_keys),
        grid=(T // tm,),
        in_specs=[pl.BlockSpec((tm, QW), lambda i: (i, 0)),
                  pl.BlockSpec(k_sort.shape, lambda i: (0, 0, 0)),
                  pl.BlockSpec(k_head.shape, lambda i: (0, 0))],
        out_specs=[pl.BlockSpec((rows, tm), lambda i: (0, i)),
                   pl.BlockSpec((rows, tm), lambda i: (0, i)),
                   pl.BlockSpec((heads, tm), lambda i: (0, i)),
                   pl.BlockSpec((heads, tm), lambda i: (0, i))],
        out_shape=[jax.ShapeDtypeStruct((rows, T), F32),
                   jax.ShapeDtypeStruct((rows, T), F32),
                   jax.ShapeDtypeStruct((heads, T), F32),
                   jax.ShapeDtypeStruct((heads, T), F32)],
        scratch_shapes=[pltpu.VMEM((rows, tm), F32)],
        compiler_params=_cparams(("parallel",), vmem),
        name="peer_select",
    )(q, k_sort, k_head)


def _peer_main_kernel(h_ref, u_ref, v_ref, s0_ref, s1_ref, tau_ref, cn_ref, o_ref, coef_scr, *, n_keys):
    e = pl.program_id(1)
    heads = tau_ref.shape[0]
    eb = u_ref.shape[0]

    @pl.when(e == 0)
    def _():
        o_ref[...] = jnp.zeros_like(o_ref)

    act_t = lax.dot_general(u_ref[...], h_ref[...], _CONTRACT_LAST, preferred_element_type=F32)
    for ii in range(eb // n_keys):
        w = None
        for h in range(heads):
            sm = s0_ref[ii * heads + h:ii * heads + h + 1, :] + s1_ref[h * n_keys:(h + 1) * n_keys, :]
            term = jnp.where(sm >= tau_ref[h:h + 1, :], jnp.exp(sm - cn_ref[h:h + 1, :]), 0.0)
            w = term if w is None else w + term
        a = act_t[ii * n_keys:(ii + 1) * n_keys, :]
        gelu = 0.5 * a * (1.0 + lax.erf(a * (2.0 ** -0.5)))
        coef_scr[ii * n_keys:(ii + 1) * n_keys, :] = (w * gelu).astype(coef_scr.dtype)
    o_ref[...] += lax.dot_general(coef_scr[...], v_ref[...], _CONTRACT_FIRST, preferred_element_type=F32)


def _peer_main(h2, u_bf, v_bf, s0, s1, tau, cn, n_keys):
    T, D = h2.shape
    NE = u_bf.shape[0]
    heads = tau.shape[0]
    rows = s1.shape[0]
    tm = _tile(T, PEER_TOKEN_TILE)
    eb = _tile(NE, 512)
    assert eb % n_keys == 0
    s0_rows = eb // n_keys * heads
    once = pl.Buffered(1)
    vmem = _nbytes((tm, D), BF16) + _nbytes((rows, tm), F32) + _nbytes((tm, D), F32) \
        + 2 * (2 * _nbytes((eb, D), BF16)) + 8 * _nbytes((eb, tm), F32)
    return pl.pallas_call(
        functools.partial(_peer_main_kernel, n_keys=n_keys),
        grid=(T // tm, NE // eb),
        in_specs=[pl.BlockSpec((tm, D), lambda i, e: (i, 0), pipeline_mode=once),
                  pl.BlockSpec((eb, D), lambda i, e: (e, 0)),
                  pl.BlockSpec((eb, D), lambda i, e: (e, 0)),
                  pl.BlockSpec((s0_rows, tm), lambda i, e: (e, i)),
                  pl.BlockSpec((rows, tm), lambda i, e: (0, i), pipeline_mode=once),
                  pl.BlockSpec((heads, tm), lambda i, e: (0, i)),
                  pl.BlockSpec((heads, tm), lambda i, e: (0, i))],
        out_specs=pl.BlockSpec((tm, D), lambda i, e: (i, 0), pipeline_mode=once),
        out_shape=jax.ShapeDtypeStruct((T, D), F32),
        scratch_shapes=[pltpu.VMEM((eb, tm), BF16)],
        compiler_params=_cparams(("parallel", "arbitrary"), vmem),
        name="peer_main",
    )(h2, u_bf, v_bf, s0, s1, tau, cn)


def _residual_kernel(x_ref, p_ref, g_ref, o_ref):
    o_ref[0] = x_ref[0] + g_ref[0] * p_ref[0]


def _residual(x1, peer_out, mod3, r0, k_gate):
    B, S, D = x1.shape
    ts = _tile(S, 512)
    vmem = 2 * 3 * _nbytes((ts, D), F32) + _nbytes((ts, D), F32)
    return pl.pallas_call(
        _residual_kernel,
        grid=(B, S // ts),
        in_specs=[pl.BlockSpec((1, ts, D), lambda b, s: (b, s, 0)),
                  pl.BlockSpec((1, ts, D), lambda b, s: (b, s, 0)),
                  pl.BlockSpec((1, 1, D), lambda b, s: (r0 + b, 0, k_gate))],
        out_specs=pl.BlockSpec((1, ts, D), lambda b, s: (b, s, 0)),
        out_shape=jax.ShapeDtypeStruct((B, S, D), F32),
        compiler_params=_cparams(("parallel", "parallel"), vmem),
        name="residual",
    )(x1, peer_out, mod3)


def _rope_tables(seq_len, hd):
    rows = jnp.repeat(jnp.arange(seq_len // GRID_W), GRID_W).astype(F32)
    cols = jnp.tile(jnp.arange(GRID_W), seq_len // GRID_W).astype(F32)
    axis_dim = hd // 2
    inv_freq = ROPE_THETA ** (-jnp.arange(0, axis_dim, 2, dtype=F32) / axis_dim)
    ang_r = rows[:, None] * inv_freq
    ang_c = cols[:, None] * inv_freq
    cos_t = jnp.concatenate([jnp.cos(ang_r), jnp.cos(ang_r), jnp.cos(ang_c), jnp.cos(ang_c)], axis=-1)
    sin_t = jnp.concatenate([-jnp.sin(ang_r), jnp.sin(ang_r), -jnp.sin(ang_c), jnp.sin(ang_c)], axis=-1)
    return cos_t, sin_t


def _block_keys(keys):
    H, _, NK, C = keys.shape
    eye = jnp.eye(H, dtype=keys.dtype)
    k_sort = jnp.einsum('hpkc,hg->pkhgc', keys, eye).reshape(2, NK * H, H * C)
    k_head = jnp.einsum('hkc,hg->hkgc', keys[:, 1], eye).reshape(H * NK, H * C)
    return k_sort.astype(BF16), k_head.astype(BF16)


def _encoder_layer(x, r0, mod3, p):
    B, S, D = x.shape
    hd, attn_w, kv_w, lru_w = p['hd'], p['attn_w'], p['kv_w'], p['lru_w']
    cos_t, sin_t = _rope_tables(S, hd)

    h = _modnorm(x, mod3, r0, p['g_norm1'], 0, 1)
    qkv = _qkv_proj(h, p['w_in'], p['gq'], p['gk'], cos_t, sin_t, attn_w, kv_w, hd)
    proj2 = _proj2(h, p['w_in'], attn_w + 2 * kv_w, lru_w)
    attn = _attention(qkv, attn_w, kv_w, hd)
    hf = _lru(proj2, p['conv_w'], p['conv_b'], p['w_gate'][0], p['b_gate'][0], p['lam'][0], lru_w,
              reverse=False)
    lru = _lru(proj2, p['conv_w'], p['conv_b'], p['w_gate'][1], p['b_gate'][1], p['lam'][1], lru_w,
               reverse=True, hf=hf)
    merged = _merge(attn, lru, p['w_attn_o'], p['w_lru_o'], proj2, lru_w)
    x1 = _outproj(merged, p['w_out'], x, mod3, r0, 2)

    h2 = _modnorm(x1, mod3, r0, p['g_norm2'], 3, 4).reshape(B * S, D)
    q = _matmul(h2, p['peer_wq'], BF16)
    s0, s1, tau, cn = _peer_select(q, p['k_sort'], p['k_head'], p['n_keys'])
    peer = _peer_main(h2, p['peer_u'], p['peer_v'], s0, s1, tau, cn, p['n_keys'])
    return _residual(x1, peer.reshape(B, S, D), mod3, r0, 5)


def kernel(x_prompt, x_sample, c_prompt, c_sample, w_ada, b_ada, g_norm1, g_norm2, w_in, q_gain, k_gain,
           conv_w, conv_b, lru_lam, lru_wa, lru_ba, lru_wi, lru_bi, w_attn_o, w_lru_o, w_out,
           peer_wq, peer_keys, peer_u, peer_v):
    depth = w_ada.shape[0]
    D = x_prompt.shape[-1]
    hd = q_gain.shape[-1]
    attn_w = w_attn_o.shape[1]
    lru_w = w_lru_o.shape[1]
    kv_w = (w_in.shape[2] - attn_w - 2 * lru_w - 2 * D) // 2
    heads, _, n_keys, half = peer_keys.shape[1:]
    assert heads == SUBLANES, "selection layout puts one expert head per sublane"

    n_p, n_s = c_prompt.shape[0], c_sample.shape[0]
    rows = -(-(n_p + n_s) // SUBLANES) * SUBLANES
    c_rows = jnp.zeros((rows, D), F32).at[:n_p].set(c_prompt).at[n_p:n_p + n_s].set(c_sample)

    y_p, y_s = x_prompt, x_sample
    for l in range(depth):
        wq = peer_wq[l].reshape(D, heads, 2, half).transpose(0, 2, 1, 3).reshape(D, 2 * heads * half)
        k_sort, k_head = _block_keys(peer_keys[l])
        p = dict(
            hd=hd, attn_w=attn_w, kv_w=kv_w, lru_w=lru_w, n_keys=n_keys,
            g_norm1=g_norm1[l], g_norm2=g_norm2[l],
            w_in=w_in[l].astype(BF16),
            gq=(q_gain[l] * (hd ** -0.5 * math.log2(math.e))).reshape(1, hd), gk=k_gain[l].reshape(1, hd),
            conv_w=conv_w[l], conv_b=conv_b[l], lam=lru_lam[l],
            w_gate=jnp.concatenate([lru_wa[l], lru_wi[l]], axis=-1).astype(BF16),
            b_gate=jnp.stack([lru_ba[l], lru_bi[l]], axis=1),
            w_attn_o=w_attn_o[l].astype(BF16), w_lru_o=w_lru_o[l].astype(BF16), w_out=w_out[l].astype(BF16),
            peer_wq=wq.astype(BF16), k_sort=k_sort, k_head=k_head,
            peer_u=peer_u[l].astype(BF16), peer_v=peer_v[l].astype(BF16),
        )
        mod = _adaln(c_rows, w_ada[l], b_ada[l])
        mod3 = mod.reshape(rows, 1, N_MOD * D)
        y_p = _encoder_layer(y_p, 0, mod3, p)
        y_s = _encoder_layer(y_s, n_p, mod3, p)
    return (y_p, y_s)
```

```python
import functools
import math

import jax
import jax.numpy as jnp
from jax import lax
from jax.experimental import pallas as pl
from jax.experimental.pallas import tpu as pltpu

F32 = jnp.float32
BF16 = jnp.bfloat16

EPS = 1e-6
ROPE_THETA = 10000.0
GRID_W = 64
LRU_C = 8.0
CONV_LEFT = 2
PEER_TOPK = 16
N_MOD = 6

V7X_VMEM_BYTES = 64 * 1024 * 1024
V7X_VMEM_RESERVE = 8 * 1024 * 1024
LANES = 128
SUBLANES = 8

LRU_TIME_CHUNK = 1024
ATTN_Q_TILE = 512
ATTN_KV_TILE = 1024
PEER_TOKEN_TILE = 1024

_CONTRACT_LAST = (((1,), (1,)), ((), ()))
_CONTRACT_FIRST = (((0,), (0,)), ((), ()))


def _cparams(semantics, vmem_bytes):
    limit = min(int(vmem_bytes), V7X_VMEM_BYTES - V7X_VMEM_RESERVE)
    return pltpu.CompilerParams(dimension_semantics=semantics, vmem_limit_bytes=limit)


def _tile(n, pref, *also):
    g = math.gcd(n, *also)
    for t in range(min(pref, g) // LANES * LANES, 0, -LANES):
        if g % t == 0:
            return t
    return g


def _nbytes(shape, dtype):
    return math.prod(shape) * jnp.dtype(dtype).itemsize


def _ada_kernel(c_ref, w_ref, b_ref, o_ref):
    c = c_ref[...]
    s = c * jax.nn.sigmoid(c)
    o_ref[...] = jnp.dot(s, w_ref[...], preferred_element_type=F32,
                         precision=lax.Precision.HIGHEST) + b_ref[...]


def _adaln(c_rows, w_ada, b_ada):
    R, D = c_rows.shape
    N = w_ada.shape[1]
    tn = _tile(N, 512)
    vmem = 2 * (_nbytes((R, D), F32) + _nbytes((D, tn), F32) + 2 * _nbytes((R, tn), F32)) \
        + 3 * _nbytes((D, tn), F32)
    return pl.pallas_call(
        _ada_kernel,
        grid=(N // tn,),
        in_specs=[pl.BlockSpec((R, D), lambda j: (0, 0)),
                  pl.BlockSpec((D, tn), lambda j: (0, j)),
                  pl.BlockSpec((1, tn), lambda j: (0, j))],
        out_specs=pl.BlockSpec((R, tn), lambda j: (0, j)),
        out_shape=jax.ShapeDtypeStruct((R, N), F32),
        compiler_params=_cparams(("parallel",), vmem),
        name="adaln",
    )(c_rows, w_ada, b_ada.reshape(1, N))


def _modnorm_kernel(x_ref, g_ref, sh_ref, sc_ref, o_ref):
    x = x_ref[0]
    y = x * lax.rsqrt(jnp.mean(x * x, axis=-1, keepdims=True) + EPS) * g_ref[...]
    o_ref[0] = (y * (1.0 + sc_ref[0]) + sh_ref[0]).astype(o_ref.dtype)


def _modnorm(x, mod3, r0, g, k_shift, k_scale):
    B, S, D = x.shape
    ts = _tile(S, 512)
    vmem = 2 * (_nbytes((ts, D), F32) + _nbytes((ts, D), BF16)) + 3 * _nbytes((ts, D), F32)
    return pl.pallas_call(
        _modnorm_kernel,
        grid=(B, S // ts),
        in_specs=[pl.BlockSpec((1, ts, D), lambda b, s: (b, s, 0)),
                  pl.BlockSpec((1, D), lambda b, s: (0, 0)),
                  pl.BlockSpec((1, 1, D), lambda b, s: (r0 + b, 0, k_shift)),
                  pl.BlockSpec((1, 1, D), lambda b, s: (r0 + b, 0, k_scale))],
        out_specs=pl.BlockSpec((1, ts, D), lambda b, s: (b, s, 0)),
        out_shape=jax.ShapeDtypeStruct((B, S, D), BF16),
        compiler_params=_cparams(("parallel", "parallel"), vmem),
        name="modnorm",
    )(x, g.reshape(1, D), mod3, mod3)


def _qkv_kernel(h_ref, w_ref, gq_ref, gk_ref, cos_ref, sin_ref, o_ref, *, n_q, n_qk, hd):
    j = pl.program_id(2)
    acc = jnp.dot(h_ref[0], w_ref[...], preferred_element_type=F32)
    tn = acc.shape[1]

    @pl.when(j < n_qk)
    def _():
        gain = jnp.where(j < n_q, gq_ref[...], gk_ref[...])
        cos = cos_ref[...]
        sin = sin_ref[...]
        lane = lax.broadcasted_iota(jnp.int32, (1, hd), 1)
        first = (lane % (hd // 2)) < (hd // 4)
        for hh in range(tn // hd):
            y = acc[:, hh * hd:(hh + 1) * hd]
            y = y * lax.rsqrt(jnp.mean(y * y, axis=-1, keepdims=True) + EPS) * gain
            sw = jnp.where(first, pltpu.roll(y, hd - hd // 4, 1), pltpu.roll(y, hd // 4, 1))
            o_ref[0, hh] = (y * cos + sw * sin).astype(o_ref.dtype)

    @pl.when(j >= n_qk)
    def _():
        for hh in range(tn // hd):
            o_ref[0, hh] = acc[:, hh * hd:(hh + 1) * hd].astype(o_ref.dtype)


def _qkv_proj(h, w_in_bf, gq, gk, cos_t, sin_t, attn_w, kv_w, hd):
    B, S, D = h.shape
    n_cols = attn_w + 2 * kv_w
    tn = _tile(attn_w, 512, kv_w)
    tm = _tile(S, 1024)
    vmem = 2 * (_nbytes((tm, D), BF16) + _nbytes((D, tn), BF16) + _nbytes((tm, tn), BF16)
                + 2 * _nbytes((tm, hd), F32)) + 4 * _nbytes((tm, tn), F32)
    kern = functools.partial(_qkv_kernel, n_q=attn_w // tn, n_qk=(attn_w + kv_w) // tn, hd=hd)
    return pl.pallas_call(
        kern,
        grid=(B, S // tm, n_cols // tn),
        in_specs=[pl.BlockSpec((1, tm, D), lambda b, i, j: (b, i, 0)),
                  pl.BlockSpec((D, tn), lambda b, i, j: (0, j)),
                  pl.BlockSpec((1, hd), lambda b, i, j: (0, 0)),
                  pl.BlockSpec((1, hd), lambda b, i, j: (0, 0)),
                  pl.BlockSpec((tm, hd), lambda b, i, j: (i, 0)),
                  pl.BlockSpec((tm, hd), lambda b, i, j: (i, 0))],
        out_specs=pl.BlockSpec((1, tn // hd, tm, hd), lambda b, i, j: (b, j, i, 0)),
        out_shape=jax.ShapeDtypeStruct((B, n_cols // hd, S, hd), BF16),
        compiler_params=_cparams(("parallel", "parallel", "arbitrary"), vmem),
        name="qkv_proj",
    )(h, w_in_bf, gq, gk, cos_t, sin_t)


def _proj2_kernel(h_ref, w_ref, o_ref, *, n_plain):
    j = pl.program_id(2)
    acc = jnp.dot(h_ref[0], w_ref[...], preferred_element_type=F32)

    @pl.when(j < n_plain)
    def _():
        o_ref[0] = acc

    @pl.when(j >= n_plain)
    def _():
        o_ref[0] = jax.nn.sigmoid(acc)


def _proj2(h, w_in_bf, col0, lru_w):
    B, S, D = h.shape
    n_cols = 2 * lru_w + 2 * D
    tn = _tile(2 * lru_w, 1024, 2 * D, col0)
    tm = _tile(S, 1024)
    j0 = col0 // tn
    vmem = 2 * (_nbytes((tm, D), BF16) + _nbytes((D, tn), BF16) + _nbytes((tm, tn), F32)) \
        + 4 * _nbytes((tm, tn), F32)
    return pl.pallas_call(
        functools.partial(_proj2_kernel, n_plain=2 * lru_w // tn),
        grid=(B, S // tm, n_cols // tn),
        in_specs=[pl.BlockSpec((1, tm, D), lambda b, i, j: (b, i, 0)),
                  pl.BlockSpec((D, tn), lambda b, i, j: (0, j0 + j))],
        out_specs=pl.BlockSpec((1, tm, tn), lambda b, i, j: (b, i, j)),
        out_shape=jax.ShapeDtypeStruct((B, S, n_cols), F32),
        compiler_params=_cparams(("parallel", "parallel", "arbitrary"), vmem),
        name="proj2",
    )(h, w_in_bf)


def _attn_kernel(q_ref, k_ref, v_ref, o_ref, m_scr, l_scr, acc_scr, *, tk):
    _, G, tq, hd = q_ref.shape
    rows = G * tq
    nk = k_ref.shape[2] // tk
    q = q_ref[0].reshape(rows, hd)
    m_scr[...] = jnp.full_like(m_scr, -jnp.inf)
    l_scr[...] = jnp.zeros_like(l_scr)
    acc_scr[...] = jnp.zeros_like(acc_scr)

    def body(c, carry):
        off = pl.multiple_of(c * tk, tk)
        k = k_ref[0, 0, pl.ds(off, tk), :]
        v = v_ref[0, 0, pl.ds(off, tk), :]
        s = lax.dot_general(q, k, _CONTRACT_LAST, preferred_element_type=F32)
        m_prev = m_scr[...]
        m_new = jnp.maximum(m_prev, jnp.max(s, axis=-1, keepdims=True))
        alpha = jnp.exp2(m_prev - m_new)
        p = jnp.exp2(s - jnp.concatenate([m_new] * (tk // LANES), axis=1))
        l_scr[...] = alpha * l_scr[...] + jnp.sum(p, axis=-1, keepdims=True)
        acc_scr[...] = alpha * acc_scr[...] + jnp.dot(p.astype(v.dtype), v, preferred_element_type=F32)
        m_scr[...] = m_new
        return carry

    lax.fori_loop(0, nk, body, 0)
    out = acc_scr[...] / l_scr[...]
    for g in range(G):
        o_ref[0, :, g * hd:(g + 1) * hd] = out[g * tq:(g + 1) * tq].astype(o_ref.dtype)


def _attention(qkv, attn_w, kv_w, hd):
    B, _, S, _ = qkv.shape
    assert hd == LANES, "running max / sum scratch is kept lane-replicated at head width"
    kvh = kv_w // hd
    G = attn_w // kv_w
    tq = _tile(S, ATTN_Q_TILE)
    tk = _tile(S, ATTN_KV_TILE)
    k0 = attn_w // hd
    v0 = (attn_w + kv_w) // hd
    rows = G * tq
    vmem = 2 * (2 * _nbytes((rows, hd), BF16) + 2 * _nbytes((S, hd), BF16)) + 3 * _nbytes((rows, hd), F32) \
        + 4 * _nbytes((rows, tk), F32)
    return pl.pallas_call(
        functools.partial(_attn_kernel, tk=tk),
        grid=(B, kvh, S // tq),
        in_specs=[pl.BlockSpec((1, G, tq, hd), lambda b, n, i: (b, n, i, 0)),
                  pl.BlockSpec((1, 1, S, hd), lambda b, n, i: (b, k0 + n, 0, 0)),
                  pl.BlockSpec((1, 1, S, hd), lambda b, n, i: (b, v0 + n, 0, 0))],
        out_specs=pl.BlockSpec((1, tq, G * hd), lambda b, n, i: (b, i, n)),
        out_shape=jax.ShapeDtypeStruct((B, S, attn_w), BF16),
        scratch_shapes=[pltpu.VMEM((rows, hd), F32)] * 3,
        compiler_params=_cparams(("parallel", "parallel", "arbitrary"), vmem),
        name="attention",
    )(qkv, qkv, qkv)


def _lru_kernel(*refs, reverse, n_chunks, conv_taps):
    if reverse:
        (cur_ref, prev_ref, next_ref, cw_ref, cb_ref, w_ref, b_ref, lam_ref, hf_ref, yl_ref,
         o_ref, ext_scr, a_scr, u_scr, carry_scr) = refs
    else:
        (cur_ref, prev_ref, next_ref, cw_ref, cb_ref, w_ref, b_ref, lam_ref,
         o_ref, ext_scr, a_scr, u_scr, carry_scr) = refs
    tc, wc = a_scr.shape
    bw = w_ref.shape[1]
    s = pl.program_id(2)
    chunk = (n_chunks - 1 - s) if reverse else s

    ext_scr[0:SUBLANES, :] = jnp.where(chunk > 0, prev_ref[0], 0.0)
    ext_scr[SUBLANES:SUBLANES + tc, :] = cur_ref[0]
    ext_scr[SUBLANES + tc:, :] = jnp.where(chunk < n_chunks - 1, next_ref[0], 0.0)
    xc = cb_ref[...]
    for j in range(conv_taps):
        r0 = SUBLANES - CONV_LEFT + j
        xc = xc + ext_scr[r0:r0 + tc, :] * cw_ref[j:j + 1, :]

    lam = -lam_ref[...]
    softplus_neg_lam = jnp.maximum(lam, 0.0) + jnp.log1p(jnp.exp(-jnp.abs(lam)))
    for n in range(wc // bw):
        cols = slice(n * bw, (n + 1) * bw)
        xb = xc[:, cols]
        z = jnp.dot(xb.astype(BF16), w_ref[n], preferred_element_type=F32)
        r = jax.nn.sigmoid(z[:, :bw] + b_ref[0:1, cols])
        i = jax.nn.sigmoid(z[:, bw:] + b_ref[1:2, cols])
        log_a = -LRU_C * r * softplus_neg_lam[:, cols]
        a = jnp.exp(log_a)
        mult = jnp.sqrt(jnp.maximum(-jnp.tanh(log_a) * (a * a + 1.0), 0.0))
        a_scr[:, cols] = a
        u_scr[:, cols] = mult * (i * xb)

    @pl.when(s == 0)
    def _():
        carry_scr[...] = jnp.zeros_like(carry_scr)

    row = lax.broadcasted_iota(jnp.int32, (SUBLANES, wc), 0)
    n_tiles = tc // SUBLANES

    def tile_body(t, h_in):
        tile = (n_tiles - 1 - t) if reverse else t
        r0 = pl.multiple_of(tile * SUBLANES, SUBLANES)
        a = a_scr[pl.ds(r0, SUBLANES), :]
        u = u_scr[pl.ds(r0, SUBLANES), :]
        for d in (1, 2, 4):
            if reverse:
                valid = row < SUBLANES - d
                shift = SUBLANES - d
            else:
                valid = row >= d
                shift = d
            u = u + a * jnp.where(valid, pltpu.roll(u, shift, 0), 0.0)
            a = a * jnp.where(valid, pltpu.roll(a, shift, 0), 1.0)
        h = u + a * h_in
        u_scr[pl.ds(r0, SUBLANES), :] = h
        edge = h[0:1, :] if reverse else h[SUBLANES - 1:SUBLANES, :]
        return jnp.broadcast_to(edge, (SUBLANES, wc))

    carry_scr[...] = lax.fori_loop(0, n_tiles, tile_body, carry_scr[...], unroll=2)

    if reverse:
        y = yl_ref[0]
        gelu = 0.5 * y * (1.0 + lax.erf(y * (2.0 ** -0.5)))
        o_ref[0] = ((hf_ref[0] + u_scr[...]) * gelu).astype(o_ref.dtype)
    else:
        o_ref[0] = u_scr[...]


def _lru(proj2, conv_w, conv_b, w_gate, b_gate, lam, lru_w, *, reverse, hf=None):
    B, S, _ = proj2.shape
    nb, bw, _ = w_gate.shape
    taps = conv_w.shape[0]
    wc = _tile(lru_w, 512)
    tc = _tile(S, LRU_TIME_CHUNK)
    n_chunks = S // tc
    n_wc = lru_w // wc
    halo_blocks = S // SUBLANES
    per_chunk = tc // SUBLANES

    def t_idx(s):
        return (n_chunks - 1 - s) if reverse else s

    in_specs = [
        pl.BlockSpec((1, tc, wc), lambda b, c, s: (b, t_idx(s), c)),
        pl.BlockSpec((1, SUBLANES, wc), lambda b, c, s: (b, jnp.maximum(t_idx(s) * per_chunk - 1, 0), c)),
        pl.BlockSpec((1, SUBLANES, wc),
                     lambda b, c, s: (b, jnp.minimum((t_idx(s) + 1) * per_chunk, halo_blocks - 1), c)),
        pl.BlockSpec((taps, wc), lambda b, c, s: (0, c)),
        pl.BlockSpec((1, wc), lambda b, c, s: (0, c)),
        pl.BlockSpec((wc // bw, bw, 2 * bw), lambda b, c, s: (c, 0, 0)),
        pl.BlockSpec((2, wc), lambda b, c, s: (0, c)),
        pl.BlockSpec((1, wc), lambda b, c, s: (0, c)),
    ]
    args = [proj2, proj2, proj2, conv_w, conv_b.reshape(1, lru_w), w_gate, b_gate, lam.reshape(1, lru_w)]
    if reverse:
        in_specs += [pl.BlockSpec((1, tc, wc), lambda b, c, s: (b, t_idx(s), c)),
                     pl.BlockSpec((1, tc, wc), lambda b, c, s: (b, t_idx(s), n_wc + c))]
        args += [hf, proj2]
        out_dtype = BF16
    else:
        out_dtype = F32
    blk = _nbytes((tc, wc), F32)
    vmem = 2 * (4 * blk) + 3 * blk + 10 * blk
    return pl.pallas_call(
        functools.partial(_lru_kernel, reverse=reverse, n_chunks=n_chunks, conv_taps=taps),
        grid=(B, n_wc, n_chunks),
        in_specs=in_specs,
        out_specs=pl.BlockSpec((1, tc, wc), lambda b, c, s: (b, t_idx(s), c)),
        out_shape=jax.ShapeDtypeStruct((B, S, lru_w), out_dtype),
        scratch_shapes=[pltpu.VMEM((tc + 2 * SUBLANES, wc), F32),
                        pltpu.VMEM((tc, wc), F32),
                        pltpu.VMEM((tc, wc), F32),
                        pltpu.VMEM((SUBLANES, wc), F32)],
        compiler_params=_cparams(("parallel", "parallel", "arbitrary"), vmem),
        name="lru_bwd" if reverse else "lru_fwd",
    )(*args)


def _merge_kernel(attn_ref, lru_ref, wa_ref, wl_ref, ga_ref, gl_ref, o_ref):
    a = jnp.dot(attn_ref[0], wa_ref[...], preferred_element_type=F32)
    l = jnp.dot(lru_ref[0], wl_ref[...], preferred_element_type=F32)
    o_ref[0] = (ga_ref[0] * a + gl_ref[0] * l).astype(o_ref.dtype)


def _merge(attn, lru, w_attn_o_bf, w_lru_o_bf, proj2, lru_w):
    B, S, attn_w = attn.shape
    D = w_attn_o_bf.shape[1]
    tm = _tile(S, 1024)
    tn = _tile(D, 512, 2 * lru_w)
    ga0 = 2 * lru_w // tn
    gl0 = (2 * lru_w + D) // tn
    vmem = 2 * (_nbytes((tm, attn_w), BF16) + _nbytes((tm, lru_w), BF16) + _nbytes((attn_w, tn), BF16)
                + _nbytes((lru_w, tn), BF16) + 2 * _nbytes((tm, tn), F32) + _nbytes((tm, tn), BF16)) \
        + 3 * _nbytes((tm, tn), F32)
    return pl.pallas_call(
        _merge_kernel,
        grid=(B, S // tm, D // tn),
        in_specs=[pl.BlockSpec((1, tm, attn_w), lambda b, i, j: (b, i, 0)),
                  pl.BlockSpec((1, tm, lru_w), lambda b, i, j: (b, i, 0)),
                  pl.BlockSpec((attn_w, tn), lambda b, i, j: (0, j)),
                  pl.BlockSpec((lru_w, tn), lambda b, i, j: (0, j)),
                  pl.BlockSpec((1, tm, tn), lambda b, i, j: (b, i, ga0 + j)),
                  pl.BlockSpec((1, tm, tn), lambda b, i, j: (b, i, gl0 + j))],
        out_specs=pl.BlockSpec((1, tm, tn), lambda b, i, j: (b, i, j)),
        out_shape=jax.ShapeDtypeStruct((B, S, D), BF16),
        compiler_params=_cparams(("parallel", "parallel", "arbitrary"), vmem),
        name="merge",
    )(attn, lru, w_attn_o_bf, w_lru_o_bf, proj2, proj2)


def _outproj_kernel(m_ref, w_ref, x_ref, g_ref, o_ref):
    o_ref[0] = x_ref[0] + g_ref[0] * jnp.dot(m_ref[0], w_ref[...], preferred_element_type=F32)


def _outproj(merged, w_out_bf, x, mod3, r0, k_gate):
    B, S, D = x.shape
    tm = _tile(S, 1024)
    tn = _tile(D, 1024)
    gpb = D // tn
    vmem = 2 * (_nbytes((tm, D), BF16) + _nbytes((D, tn), BF16) + 2 * _nbytes((tm, tn), F32)) \
        + 2 * _nbytes((tm, tn), F32)
    return pl.pallas_call(
        _outproj_kernel,
        grid=(B, S // tm, D // tn),
        in_specs=[pl.BlockSpec((1, tm, D), lambda b, i, j: (b, i, 0)),
                  pl.BlockSpec((D, tn), lambda b, i, j: (0, j)),
                  pl.BlockSpec((1, tm, tn), lambda b, i, j: (b, i, j)),
                  pl.BlockSpec((1, 1, tn), lambda b, i, j: (r0 + b, 0, k_gate * gpb + j))],
        out_specs=pl.BlockSpec((1, tm, tn), lambda b, i, j: (b, i, j)),
        out_shape=jax.ShapeDtypeStruct((B, S, D), F32),
        compiler_params=_cparams(("parallel", "parallel", "arbitrary"), vmem),
        name="outproj",
    )(merged, w_out_bf, x, mod3)


def _matmul_kernel(a_ref, w_ref, o_ref):
    o_ref[...] = jnp.dot(a_ref[...], w_ref[...], preferred_element_type=F32).astype(o_ref.dtype)


def _matmul(a, w, out_dtype):
    M, K = a.shape
    N = w.shape[1]
    tm = _tile(M, 1024)
    tn = _tile(N, 1024)
    vmem = 2 * (_nbytes((tm, K), a.dtype) + _nbytes((K, tn), w.dtype) + _nbytes((tm, tn), out_dtype)) \
        + 2 * _nbytes((tm, tn), F32)
    return pl.pallas_call(
        _matmul_kernel,
        grid=(M // tm, N // tn),
        in_specs=[pl.BlockSpec((tm, K), lambda i, j: (i, 0)),
                  pl.BlockSpec((K, tn), lambda i, j: (0, j))],
        out_specs=pl.BlockSpec((tm, tn), lambda i, j: (i, j)),
        out_shape=jax.ShapeDtypeStruct((M, N), out_dtype),
        compiler_params=_cparams(("parallel", "arbitrary"), vmem),
        name="peer_query",
    )(a, w)


def _batcher_pairs(n):
    pairs = []

    def merge(lo, hi, r):
        step = r * 2
        if step < hi - lo:
            merge(lo, hi, step)
            merge(lo + r, hi, step)
            pairs.extend((i, i + r) for i in range(lo + r, hi - r, step))
        else:
            pairs.append((lo, lo + r))

    def sort(lo, hi):
        if hi - lo >= 1:
            mid = lo + (hi - lo) // 2
            sort(lo, mid)
            sort(mid + 1, hi)
            merge(lo, hi, 1)

    sort(0, n - 1)
    return pairs


_SORT_PAIRS = _batcher_pairs(PEER_TOPK)


def _sort_desc(v):
    v = list(v)
    for i, j in _SORT_PAIRS:
        v[i], v[j] = jnp.maximum(v[i], v[j]), jnp.minimum(v[i], v[j])
    return v


def _merge_top(a, b):
    k = len(a)
    c = [jnp.maximum(a[i], b[k - 1 - i]) for i in range(k)]
    d = k // 2
    while d >= 1:
        for i in range(k):
            if not i & d:
                c[i], c[i + d] = jnp.maximum(c[i], c[i + d]), jnp.minimum(c[i], c[i + d])
        d //= 2
    return c


def _top_sorted(vals):
    k = PEER_TOPK
    lists = [_sort_desc(vals[g:g + k]) for g in range(0, len(vals), k)]
    while len(lists) > 1:
        nxt = [_merge_top(lists[i], lists[i + 1]) for i in range(0, len(lists) - 1, 2)]
        if len(lists) % 2:
            nxt.append(lists[-1])
        lists = nxt
    return lists[0]


_CAND_PAIRS = [(a, b) for a in range(PEER_TOPK) for b in range(PEER_TOPK) if (a + 1) * (b + 1) <= PEER_TOPK]


def _peer_select_kernel(q_ref, ks_ref, kh_ref, s0_ref, s1_ref, tau_ref, cn_ref, s1_scr, *, n_keys):
    half = ks_ref.shape[2]
    q0 = q_ref[:, :half]
    q1 = q_ref[:, half:]
    s0_ref[...] = lax.dot_general(ks_ref[0], q0, _CONTRACT_LAST, preferred_element_type=F32)
    s1_scr[...] = lax.dot_general(ks_ref[1], q1, _CONTRACT_LAST, preferred_element_type=F32)
    s1_ref[...] = lax.dot_general(kh_ref[...], q1, _CONTRACT_LAST, preferred_element_type=F32)

    def lane_tile(t, carry):
        lanes = pl.ds(pl.multiple_of(t * LANES, LANES), LANES)
        tops = []
        for s_ref in (s0_ref, s1_scr):
            vals = [s_ref[v * SUBLANES:(v + 1) * SUBLANES, lanes] for v in range(n_keys)]
            tops.append(_top_sorted(vals))
        cands = [tops[0][a] + tops[1][b] for a, b in _CAND_PAIRS]
        pad = [jnp.full_like(cands[0], -jnp.inf)] * (-len(cands) % PEER_TOPK)
        tau = _top_sorted(cands + pad)[PEER_TOPK - 1]
        top = tops[0][0] + tops[1][0]
        z = jnp.zeros_like(top)
        for c in cands:
            z = z + jnp.where(c >= tau, jnp.exp(c - top), 0.0)
        tau_ref[:, lanes] = tau
        cn_ref[:, lanes] = top + jnp.log(z)
        return carry

    lax.fori_loop(0, q_ref.shape[0] // LANES, lane_tile, 0)


def _peer_select(q, k_sort, k_head, n_keys):
    T, QW = q.shape
    rows = k_sort.shape[1]
    heads = rows // n_keys
    tm = _tile(T, 512)
    vmem = 2 * (_nbytes((tm, QW), BF16) + _nbytes(k_sort.shape, BF16) + _nbytes(k_head.shape, BF16)
                + 2 * _nbytes((rows, tm), F32) + 2 * _nbytes((heads, tm), F32)) + 4 * _nbytes((rows, tm), F32)
    return pl.pallas_call(
        functools.partial(_peer_select_kernel, n_keys=n_keys),
        grid=(T // tm,),
        in_specs=[pl.BlockSpec((tm, QW), lambda i: (i, 0)),
                  pl.BlockSpec(k_sort.shape, lambda i: (0, 0, 0)),
                  pl.BlockSpec(k_head.shape, lambda i: (0, 0))],
        out_specs=[pl.BlockSpec((rows, tm), lambda i: (0, i)),
                   pl.BlockSpec((rows, tm), lambda i: (0, i)),
                   pl.BlockSpec((heads, tm), lambda i: (0, i)),
                   pl.BlockSpec((heads, tm), lambda i: (0, i))],
        out_shape=[jax.ShapeDtypeStruct((rows, T), F32),
                   jax.ShapeDtypeStruct((rows, T), F32),
                   jax.ShapeDtypeStruct((heads, T), F32),
                   jax.ShapeDtypeStruct((heads, T), F32)],
        scratch_shapes=[pltpu.VMEM((rows, tm), F32)],
        compiler_params=_cparams(("parallel",), vmem),
        name="peer_select",
    )(q, k_sort, k_head)


def _peer_main_kernel(h_ref, u_ref, v_ref, s0_ref, s1_ref, tau_ref, cn_ref, o_ref, coef_scr, *, n_keys):
    e = pl.program_id(1)
    heads = tau_ref.shape[0]
    eb = u_ref.shape[0]

    @pl.when(e == 0)
    def _():
        o_ref[...] = jnp.zeros_like(o_ref)

    act_t = lax.dot_general(u_ref[...], h_ref[...], _CONTRACT_LAST, preferred_element_type=F32)
    for ii in range(eb // n_keys):
        w = None
        for h in range(heads):
            sm = s0_ref[ii * heads + h:ii * heads + h + 1, :] + s1_ref[h * n_keys:(h + 1) * n_keys, :]
            term = jnp.where(sm >= tau_ref[h:h + 1, :], jnp.exp(sm - cn_ref[h:h + 1, :]), 0.0)
            w = term if w is None else w + term
        a = act_t[ii * n_keys:(ii + 1) * n_keys, :]
        gelu = 0.5 * a * (1.0 + lax.erf(a * (2.0 ** -0.5)))
        coef_scr[ii * n_keys:(ii + 1) * n_keys, :] = (w * gelu).astype(coef_scr.dtype)
    o_ref[...] += lax.dot_general(coef_scr[...], v_ref[...], _CONTRACT_FIRST, preferred_element_type=F32)


def _peer_main(h2, u_bf, v_bf, s0, s1, tau, cn, n_keys):
    T, D = h2.shape
    NE = u_bf.shape[0]
    heads = tau.shape[0]
    rows = s1.shape[0]
    tm = _tile(T, PEER_TOKEN_TILE)
    eb = _tile(NE, 512)
    assert eb % n_keys == 0
    s0_rows = eb // n_keys * heads
    once = pl.Buffered(1)
    vmem = _nbytes((tm, D), BF16) + _nbytes((rows, tm), F32) + _nbytes((tm, D), F32) \
        + 2 * (2 * _nbytes((eb, D), BF16)) + 8 * _nbytes((eb, tm), F32)
    return pl.pallas_call(
        functools.partial(_peer_main_kernel, n_keys=n_keys),
        grid=(T // tm, NE // eb),
        in_specs=[pl.BlockSpec((tm, D), lambda i, e: (i, 0), pipeline_mode=once),
                  pl.BlockSpec((eb, D), lambda i, e: (e, 0)),
                  pl.BlockSpec((eb, D), lambda i, e: (e, 0)),
                  pl.BlockSpec((s0_rows, tm), lambda i, e: (e, i)),
                  pl.BlockSpec((rows, tm), lambda i, e: (0, i), pipeline_mode=once),
                  pl.BlockSpec((heads, tm), lambda i, e: (0, i)),
                  pl.BlockSpec((heads, tm), lambda i, e: (0, i))],
        out_specs=pl.BlockSpec((tm, D), lambda i, e: (i, 0), pipeline_mode=once),
        out_shape=jax.ShapeDtypeStruct((T, D), F32),
        scratch_shapes=[pltpu.VMEM((eb, tm), BF16)],
        compiler_params=_cparams(("parallel", "arbitrary"), vmem),
        name="peer_main",
    )(h2, u_bf, v_bf, s0, s1, tau, cn)


def _residual_kernel(x_ref, p_ref, g_ref, o_ref):
    o_ref[0] = x_ref[0] + g_ref[0] * p_ref[0]


def _residual(x1, peer_out, mod3, r0, k_gate):
    B, S, D = x1.shape
    ts = _tile(S, 512)
    vmem = 2 * 3 * _nbytes((ts, D), F32) + _nbytes((ts, D), F32)
    return pl.pallas_call(
        _residual_kernel,
        grid=(B, S // ts),
        in_specs=[pl.BlockSpec((1, ts, D), lambda b, s: (b, s, 0)),
                  pl.BlockSpec((1, ts, D), lambda b, s: (b, s, 0)),
                  pl.BlockSpec((1, 1, D), lambda b, s: (r0 + b, 0, k_gate))],
        out_specs=pl.BlockSpec((1, ts, D), lambda b, s: (b, s, 0)),
        out_shape=jax.ShapeDtypeStruct((B, S, D), F32),
        compiler_params=_cparams(("parallel", "parallel"), vmem),
        name="residual",
    )(x1, peer_out, mod3)


def _rope_tables(seq_len, hd):
    rows = jnp.repeat(jnp.arange(seq_len // GRID_W), GRID_W).astype(F32)
    cols = jnp.tile(jnp.arange(GRID_W), seq_len // GRID_W).astype(F32)
    axis_dim = hd // 2
    inv_freq = ROPE_THETA ** (-jnp.arange(0, axis_dim, 2, dtype=F32) / axis_dim)
    ang_r = rows[:, None] * inv_freq
    ang_c = cols[:, None] * inv_freq
    cos_t = jnp.concatenate([jnp.cos(ang_r), jnp.cos(ang_r), jnp.cos(ang_c), jnp.cos(ang_c)], axis=-1)
    sin_t = jnp.concatenate([-jnp.sin(ang_r), jnp.sin(ang_r), -jnp.sin(ang_c), jnp.sin(ang_c)], axis=-1)
    return cos_t, sin_t


def _block_keys(keys):
    H, _, NK, C = keys.shape
    eye = jnp.eye(H, dtype=keys.dtype)
    k_sort = jnp.einsum('hpkc,hg->pkhgc', keys, eye).reshape(2, NK * H, H * C)
    k_head = jnp.einsum('hkc,hg->hkgc', keys[:, 1], eye).reshape(H * NK, H * C)
    return k_sort.astype(BF16), k_head.astype(BF16)


def _encoder_layer(x, r0, mod3, p):
    B, S, D = x.shape
    hd, attn_w, kv_w, lru_w = p['hd'], p['attn_w'], p['kv_w'], p['lru_w']
    cos_t, sin_t = _rope_tables(S, hd)

    h = _modnorm(x, mod3, r0, p['g_norm1'], 0, 1)
    qkv = _qkv_proj(h, p['w_in'], p['gq'], p['gk'], cos_t, sin_t, attn_w, kv_w, hd)
    proj2 = _proj2(h, p['w_in'], attn_w + 2 * kv_w, lru_w)
    attn = _attention(qkv, attn_w, kv_w, hd)
    hf = _lru(proj2, p['conv_w'], p['conv_b'], p['w_gate'][0], p['b_gate'][0], p['lam'][0], lru_w,
              reverse=False)
    lru = _lru(proj2, p['conv_w'], p['conv_b'], p['w_gate'][1], p['b_gate'][1], p['lam'][1], lru_w,
               reverse=True, hf=hf)
    merged = _merge(attn, lru, p['w_attn_o'], p['w_lru_o'], proj2, lru_w)
    x1 = _outproj(merged, p['w_out'], x, mod3, r0, 2)

    h2 = _modnorm(x1, mod3, r0, p['g_norm2'], 3, 4).reshape(B * S, D)
    q = _matmul(h2, p['peer_wq'], BF16)
    s0, s1, tau, cn = _peer_select(q, p['k_sort'], p['k_head'], p['n_keys'])
    peer = _peer_main(h2, p['peer_u'], p['peer_v'], s0, s1, tau, cn, p['n_keys'])
    return _residual(x1, peer.reshape(B, S, D), mod3, r0, 5)


def kernel(x_prompt, x_sample, c_prompt, c_sample, w_ada, b_ada, g_norm1, g_norm2, w_in, q_gain, k_gain,
           conv_w, conv_b, lru_lam, lru_wa, lru_ba, lru_wi, lru_bi, w_attn_o, w_lru_o, w_out,
           peer_wq, peer_keys, peer_u, peer_v):
    depth = w_ada.shape[0]
    D = x_prompt.shape[-1]
    hd = q_gain.shape[-1]
    attn_w = w_attn_o.shape[1]
    lru_w = w_lru_o.shape[1]
    kv_w = (w_in.shape[2] - attn_w - 2 * lru_w - 2 * D) // 2
    heads, _, n_keys, half = peer_keys.shape[1:]
    assert heads == SUBLANES, "selection layout puts one expert head per sublane"

    n_p, n_s = c_prompt.shape[0], c_sample.shape[0]
    rows = -(-(n_p + n_s) // SUBLANES) * SUBLANES
    c_rows = jnp.zeros((rows, D), F32).at[:n_p].set(c_prompt).at[n_p:n_p + n_s].set(c_sample)

    y_p, y_s = x_prompt, x_sample
    for l in range(depth):
        wq = peer_wq[l].reshape(D, heads, 2, half).transpose(0, 2, 1, 3).reshape(D, 2 * heads * half)
        k_sort, k_head = _block_keys(peer_keys[l])
        p = dict(
            hd=hd, attn_w=attn_w, kv_w=kv_w, lru_w=lru_w, n_keys=n_keys,
            g_norm1=g_norm1[l], g_norm2=g_norm2[l],
            w_in=w_in[l].astype(BF16),
            gq=(q_gain[l] * (hd ** -0.5 * math.log2(math.e))).reshape(1, hd), gk=k_gain[l].reshape(1, hd),
            conv_w=conv_w[l], conv_b=conv_b[l], lam=lru_lam[l],
            w_gate=jnp.concatenate([lru_wa[l], lru_wi[l]], axis=-1).astype(BF16),
            b_gate=jnp.stack([lru_ba[l], lru_bi[l]], axis=1),
            w_attn_o=w_attn_o[l].astype(BF16), w_lru_o=w_lru_o[l].astype(BF16), w_out=w_out[l].astype(BF16),
            peer_wq=wq.astype(BF16), k_sort=k_sort, k_head=k_head,
            peer_u=peer_u[l].astype(BF16), peer_v=peer_v[l].astype(BF16),
        )
        mod = _adaln(c_rows, w_ada[l], b_ada[l])
        mod3 = mod.reshape(rows, 1, N_MOD * D)
        y_p = _encoder_layer(y_p, 0, mod3, p)
        y_s = _encoder_layer(y_s, n_p, mod3, p)
    return (y_p, y_s)
```
